```python
import jax, jax.numpy as jnp
from jax import lax
import numpy as np

D_MODEL = 1024
BATCH = 16
SEQ = 2048
DEPTH = 4

HEAD_DIM = 64
A_HEADS = 8
A_Q_RANK = 256
A_KV_RANK = 128
A_NOPE_DIM = 64
A_ROPE_DIM = 32
A_V_DIM = 64
B_HEADS = 8
B_BRANCHES = ((128, 1), (512, 4), (2048, 16))
C_HEADS = 16
GRID_W = 64
NA_ROWS = 8
NA_COLS = 16
NA_COL_BLOCK = 16
NA_BAND = NA_COL_BLOCK + NA_COLS
D_FF = -(-8 * D_MODEL // (3 * 256)) * 256

ROPE_THETA = 10000.0
EPS = 1e-6
Q_BLOCK = 128
NEG_INF = -1e30
N_EVEN = (DEPTH + 1) // 2
N_ODD = DEPTH // 2
EVEN_IN_COLS = A_Q_RANK + A_KV_RANK + A_ROPE_DIM + 3 * B_HEADS * HEAD_DIM
EVEN_OUT_ROWS = A_HEADS * A_V_DIM + B_HEADS * HEAD_DIM
ODD_WIDTH = C_HEADS * HEAD_DIM

kernel_name = "hybrid_mla_dilated_neighbourhood_encoder"


def rmsnorm(x, g):
    xf = x.astype(jnp.float32)
    y = xf * lax.rsqrt(jnp.mean(xf * xf, axis=-1, keepdims=True) + EPS)
    return (y * g.astype(jnp.float32)).astype(x.dtype)


def rope_tables(seq_len, dim):
    pos = jnp.arange(seq_len, dtype=jnp.float32)
    inv = ROPE_THETA ** (-jnp.arange(0, dim, 2, dtype=jnp.float32) / dim)
    ang = pos[:, None] * inv[None, :]
    return jnp.cos(ang), jnp.sin(ang)


def apply_rope(x, cos, sin):
    half = x.shape[-1] // 2
    shape = (x.shape[1],) + (1,) * (x.ndim - 3) + (half,)
    c = cos.reshape(shape).astype(x.dtype)
    s = sin.reshape(shape).astype(x.dtype)
    x1, x2 = x[..., :half], x[..., half:]
    return jnp.concatenate([x1 * c - x2 * s, x2 * c + x1 * s], axis=-1)


def softmax_with_lse(s):
    m = jnp.max(s, axis=-1, keepdims=True)
    e = jnp.exp(s - m)
    den = jnp.sum(e, axis=-1, keepdims=True)
    return e / den, (m + jnp.log(den))[..., 0]


def mla_attention(q_nope, q_pe, k_nope, k_pe, v):
    B, S, H, _ = q_nope.shape
    nb = S // Q_BLOCK
    scale = (A_NOPE_DIM + A_ROPE_DIM) ** -0.5

    def blocks(t):
        return t.reshape((B, nb, Q_BLOCK) + t.shape[2:]).swapaxes(0, 1)

    def one_block(args):
        qn, qp = args
        s = (jnp.einsum('bqhd,bkhd->bhqk', qn, k_nope)
             + jnp.einsum('bqhd,bkd->bhqk', qp, k_pe)).astype(jnp.float32) * scale
        p = jax.nn.softmax(s, axis=-1).astype(v.dtype)
        return jnp.einsum('bhqk,bkhd->bqhd', p, v)

    o = lax.map(one_block, (blocks(q_nope), blocks(q_pe)))
    return o.swapaxes(0, 1).reshape(B, S, H, v.shape[-1])


def banded_attention(q, k, v, half_window, scale):
    N, L, H, dh = q.shape
    blk = half_window
    nb = -(-L // blk)
    pad = nb * blk - L
    qb = jnp.pad(q, ((0, 0), (0, pad), (0, 0), (0, 0))).reshape(N, nb, blk, H, dh)

    def windows(t):
        tb = jnp.pad(t, ((0, 0), (blk, blk + pad), (0, 0), (0, 0))).reshape(N, nb + 2, blk, H, dh)
        return jnp.concatenate([tb[:, :-2], tb[:, 1:-1], tb[:, 2:]], axis=2)

    kw, vw = windows(k), windows(v)
    qpos = np.arange(nb)[:, None] * blk + np.arange(blk)[None, :]
    kpos = (np.arange(nb)[:, None] - 1) * blk + np.arange(3 * blk)[None, :]
    kp = kpos[:, None, :]
    valid = (np.abs(kp - qpos[:, :, None]) <= half_window) & (kp >= 0) & (kp < L)
    s = jnp.einsum('nbqhd,nbkhd->nbhqk', qb, kw).astype(jnp.float32) * scale
    s = jnp.where(valid[None, :, None], s, NEG_INF)
    p, lse = softmax_with_lse(s)
    o = jnp.einsum('nbhqk,nbkhd->nbqhd', p.astype(v.dtype), vw).reshape(N, nb * blk, H, dh)[:, :L]
    lse = lse.transpose(0, 1, 3, 2).reshape(N, nb * blk, H)[:, :L]
    return o, lse


def dilated_attention(q, k, v):
    B, S, H, dh = q.shape
    scale = dh ** -0.5
    outs, lses = [], []
    for window, dilation in B_BRANCHES:
        n_sub = S // dilation

        def to_sub(t):
            return t.reshape(B, n_sub, dilation, H, dh).swapaxes(1, 2).reshape(B * dilation, n_sub, H, dh)

        o, lse = banded_attention(to_sub(q), to_sub(k), to_sub(v), window // (2 * dilation), scale)
        outs.append(o.reshape(B, dilation, n_sub, H, dh).swapaxes(1, 2).reshape(B, S, H, dh))
        lses.append(lse.reshape(B, dilation, n_sub, H).swapaxes(1, 2).reshape(B, S, H))
    w = jax.nn.softmax(jnp.stack(lses), axis=0)
    return jnp.einsum('gbsh,gbshd->bshd', w.astype(q.dtype), jnp.stack(outs))


def neighbourhood_attention(q, k, v, rpb):
    B, S, H, dh = q.shape
    rows = S // GRID_W
    kr = min(NA_ROWS, rows)
    n_cb = GRID_W // NA_COL_BLOCK
    scale = dh ** -0.5
    r = np.arange(rows)
    row_start = np.clip(r - kr // 2, 0, rows - kr)
    row_off = row_start[:, None] + np.arange(kr)[None, :] - r[:, None] + (NA_ROWS - 1)
    qcol = np.arange(GRID_W).reshape(n_cb, NA_COL_BLOCK)
    col_start = np.clip(qcol - NA_COLS // 2, 0, GRID_W - NA_COLS)
    band_start = np.clip(np.arange(n_cb) * NA_COL_BLOCK - NA_COLS // 2, 0, GRID_W - NA_BAND)
    band_col = band_start[:, None] + np.arange(NA_BAND)[None, :]
    bc = band_col[:, None, :]
    col_valid = (bc >= col_start[..., None]) & (bc < col_start[..., None] + NA_COLS)
    col_off = np.clip(bc - qcol[..., None] + (NA_COLS - 1), 0, 2 * NA_COLS - 2)
    key_valid = np.broadcast_to(col_valid[:, :, None, :], (n_cb, NA_COL_BLOCK, kr, NA_BAND)).reshape(
        n_cb, NA_COL_BLOCK, kr * NA_BAND)
    kg = k.reshape(B, rows, GRID_W, H, dh)
    vg = v.reshape(B, rows, GRID_W, H, dh)
    qg = q.reshape(B, rows, GRID_W, H, dh).swapaxes(0, 1)

    def gather_band(t_rows):
        t = t_rows[:, :, band_col]
        return t.swapaxes(1, 2).reshape(B, n_cb, kr * NA_BAND, H, dh)

    def one_row(args):
        q_row, rs, ro = args
        k_band = gather_band(lax.dynamic_slice_in_dim(kg, rs, kr, axis=1))
        v_band = gather_band(lax.dynamic_slice_in_dim(vg, rs, kr, axis=1))
        q_blk = q_row.reshape(B, n_cb, NA_COL_BLOCK, H, dh)
        s = jnp.einsum('bjqhd,bjkhd->bhjqk', q_blk, k_band).astype(jnp.float32) * scale
        bias = rpb[:, ro[:, None, None, None], col_off[None]]
        bias = bias.transpose(0, 2, 3, 1, 4).reshape(H, n_cb, NA_COL_BLOCK, kr * NA_BAND)
        s = jnp.where(key_valid, s + bias.astype(jnp.float32), NEG_INF)
        p = jax.nn.softmax(s, axis=-1).astype(v.dtype)
        o = jnp.einsum('bhjqk,bjkhd->bjqhd', p, v_band)
        return o.reshape(B, GRID_W, H, dh)

    o = lax.map(one_row, (qg, jnp.asarray(row_start, jnp.int32), jnp.asarray(row_off, jnp.int32)))
    return o.swapaxes(0, 1).reshape(B, S, H, dh)


def even_mixer(h, w_in, q_norm, w_q_up, kv_norm, w_kv_up, w_out, cos_a, sin_a, cos_b, sin_b):
    B, S, _ = h.shape
    z = h @ w_in
    c_q, c_kv, k_pe, qkv_b = jnp.split(
        z, [A_Q_RANK, A_Q_RANK + A_KV_RANK, A_Q_RANK + A_KV_RANK + A_ROPE_DIM], axis=-1)
    q_a = (rmsnorm(c_q, q_norm) @ w_q_up).reshape(B, S, A_HEADS, A_NOPE_DIM + A_ROPE_DIM)
    q_nope = q_a[..., :A_NOPE_DIM]
    q_pe = apply_rope(q_a[..., A_NOPE_DIM:], cos_a, sin_a)
    kv = (rmsnorm(c_kv, kv_norm) @ w_kv_up).reshape(B, S, A_HEADS, A_NOPE_DIM + A_V_DIM)
    k_nope, v_a = kv[..., :A_NOPE_DIM], kv[..., A_NOPE_DIM:]
    k_pe = apply_rope(k_pe, cos_a, sin_a)
    o_a = mla_attention(q_nope, q_pe, k_nope, k_pe, v_a)
    qkv_b = qkv_b.reshape(B, S, 3, B_HEADS, HEAD_DIM)
    q_b = apply_rope(qkv_b[:, :, 0], cos_b, sin_b)
    k_b = apply_rope(qkv_b[:, :, 1], cos_b, sin_b)
    o_b = dilated_attention(q_b, k_b, qkv_b[:, :, 2])
    o = jnp.concatenate([o_a.reshape(B, S, A_HEADS * A_V_DIM), o_b.reshape(B, S, B_HEADS * HEAD_DIM)], axis=-1)
    return o @ w_out


def odd_mixer(h, w_qkv, rpb, w_out):
    B, S, _ = h.shape
    qkv = (h @ w_qkv).reshape(B, S, 3, C_HEADS, HEAD_DIM)
    o = neighbourhood_attention(qkv[:, :, 0], qkv[:, :, 1], qkv[:, :, 2], rpb)
    return o.reshape(B, S, ODD_WIDTH) @ w_out


def swiglu(h, w_gate, w_up, w_down):
    return (jax.nn.silu(h @ w_gate) * (h @ w_up)) @ w_down


def setup_inputs(seed: int = 0) -> dict:
    key = jax.random.key(seed)
    ks = jax.random.split(key, 20)
    f32 = jnp.float32

    def w(k, shape, fan_in):
        return jax.random.normal(k, shape, f32) * fan_in ** -0.5

    def gain(k, shape):
        return 1.0 + 0.05 * jax.random.normal(k, shape, f32)

    return {
        "x": jax.random.normal(ks[0], (BATCH, SEQ, D_MODEL), f32),
        "ev_norm": gain(ks[1], (N_EVEN, D_MODEL)),
        "ev_w_in": w(ks[2], (N_EVEN, D_MODEL, EVEN_IN_COLS), D_MODEL),
        "ev_q_norm": gain(ks[3], (N_EVEN, A_Q_RANK)),
        "ev_w_q_up": w(ks[4], (N_EVEN, A_Q_RANK, A_HEADS * (A_NOPE_DIM + A_ROPE_DIM)), A_Q_RANK),
        "ev_kv_norm": gain(ks[5], (N_EVEN, A_KV_RANK)),
        "ev_w_kv_up": w(ks[6], (N_EVEN, A_KV_RANK, A_HEADS * (A_NOPE_DIM + A_V_DIM)), A_KV_RANK),
        "ev_w_out": w(ks[7], (N_EVEN, EVEN_OUT_ROWS, D_MODEL), EVEN_OUT_ROWS),
        "od_norm": gain(ks[8], (N_ODD, D_MODEL)),
        "od_w_qkv": w(ks[9], (N_ODD, D_MODEL, 3 * ODD_WIDTH), D_MODEL),
        "od_rpb": 0.02 * jax.random.normal(ks[10], (N_ODD, C_HEADS, 2 * NA_ROWS - 1, 2 * NA_COLS - 1), f32),
        "od_w_out": w(ks[11], (N_ODD, ODD_WIDTH, D_MODEL), ODD_WIDTH),
        "ffn_norm": gain(ks[12], (DEPTH, D_MODEL)),
        "ffn_w_gate": w(ks[13], (DEPTH, D_MODEL, D_FF), D_MODEL),
        "ffn_w_up": w(ks[14], (DEPTH, D_MODEL, D_FF), D_MODEL),
        "ffn_w_down": w(ks[15], (DEPTH, D_FF, D_MODEL), D_FF),
        "final_norm": gain(ks[16], (D_MODEL,)),
    }


def reference(x, ev_norm, ev_w_in, ev_q_norm, ev_w_q_up, ev_kv_norm, ev_w_kv_up, ev_w_out,
              od_norm, od_w_qkv, od_rpb, od_w_out,
              ffn_norm, ffn_w_gate, ffn_w_up, ffn_w_down, final_norm):
    S = x.shape[1]
    cos_a, sin_a = rope_tables(S, A_ROPE_DIM)
    cos_b, sin_b = rope_tables(S, HEAD_DIM)
    h = x
    for layer in range(DEPTH):
        i = layer // 2
        if layer % 2 == 0:
            h = h + even_mixer(rmsnorm(h, ev_norm[i]), ev_w_in[i], ev_q_norm[i], ev_w_q_up[i],
                               ev_kv_norm[i], ev_w_kv_up[i], ev_w_out[i], cos_a, sin_a, cos_b, sin_b)
        else:
            h = h + odd_mixer(rmsnorm(h, od_norm[i]), od_w_qkv[i], od_rpb[i], od_w_out[i])
        h = h + swiglu(rmsnorm(h, ffn_norm[layer]), ffn_w_gate[layer], ffn_w_up[layer], ffn_w_down[layer])
    return rmsnorm(h, final_norm)
```

```python
import functools

import numpy as np
import jax
import jax.numpy as jnp
from jax import lax
from jax.experimental import pallas as pl
from jax.experimental.pallas import tpu as pltpu

BF = jnp.bfloat16
F32 = jnp.float32

D_MODEL = 1024
DEPTH = 4
HEAD_DIM = 64
A_HEADS = 8
A_Q_RANK = 256
A_KV_RANK = 128
A_NOPE_DIM = 64
A_ROPE_DIM = 32
A_V_DIM = 64
B_HEADS = 8
B_BRANCHES = ((128, 1), (512, 4), (2048, 16))
C_HEADS = 16
GRID_W = 64
NA_ROWS = 8
NA_COLS = 16
D_FF = -(-8 * D_MODEL // (3 * 256)) * 256
ROPE_THETA = 10000.0
EPS = 1e-6
NEG_INF = -1e30

LANES = 128
B_WIDTH = B_HEADS * HEAD_DIM
C_WIDTH = C_HEADS * HEAD_DIM
A_PAD = A_HEADS * LANES
IN_COLS = A_Q_RANK + A_KV_RANK + 3 * B_WIDTH + 2 * LANES
COL_QB = A_Q_RANK + A_KV_RANK
COL_KB = COL_QB + B_WIDTH
COL_VB = COL_KB + B_WIDTH
COL_KPE = COL_VB + B_WIDTH
COL_KPE_ROT = COL_KPE + LANES
HALF_WINDOW = 64
NA_QROWS = 2
NA_KROWS = 10
NA_VARIANTS = 5
VMEM_LIMIT = 56 * 1024 * 1024


def _rms(x, g):
    return x * lax.rsqrt(jnp.mean(x * x, axis=-1, keepdims=True) + EPS) * g


def _dot(a, b):
    return jnp.dot(a, b, preferred_element_type=F32)


def _dot_nt(a, b):
    return lax.dot_general(a, b, (((1,), (1,)), ((), ())), preferred_element_type=F32)


def _const_spec(shape):
    nd = len(shape)
    return pl.BlockSpec(shape, lambda *_: (0,) * nd, pipeline_mode=pl.Buffered(1))


def _params(n_grid):
    return pltpu.CompilerParams(dimension_semantics=("arbitrary",) * n_grid,
                                vmem_limit_bytes=VMEM_LIMIT)


def _proj_even_kernel(h_ref, g_ref, win_ref, qg_ref, kvg_ref, wq_ref, wqr_ref, wk_ref, wv_ref,
                      e_ref, ta_ref, tb_ref, qa_ref, ka_ref, va_ref, qb_ref, kb_ref, vb_ref):
    xn = _rms(h_ref[...], g_ref[...]).astype(BF)
    z = _dot(xn, win_ref[...])
    cq = _rms(z[:, 0:A_Q_RANK], qg_ref[...]).astype(BF)
    ckv = _rms(z[:, A_Q_RANK:COL_QB], kvg_ref[...]).astype(BF)
    qa = _dot(cq, wq_ref[...])
    qr = _dot(cq, wqr_ref[...])
    cq_t, sq_t = ta_ref[:, 0:LANES], ta_ref[:, LANES:2 * LANES]
    ck_t, sk_t = ta_ref[:, 2 * LANES:3 * LANES], ta_ref[:, 3 * LANES:4 * LANES]
    for hd in range(A_HEADS):
        sl = slice(hd * LANES, (hd + 1) * LANES)
        qa_ref[:, sl] = (qa[:, sl] * cq_t + qr[:, sl] * sq_t).astype(BF)
    kpe = z[:, COL_KPE:COL_KPE + LANES] * ck_t + z[:, COL_KPE_ROT:COL_KPE_ROT + LANES] * sk_t
    ka_ref[...] = (_dot(ckv, wk_ref[...]) + _dot(kpe.astype(BF), e_ref[...])).astype(BF)
    va_ref[...] = _dot(ckv, wv_ref[...]).astype(BF)
    c_t, s_hi, s_lo = tb_ref[:, 0:LANES], tb_ref[:, LANES:2 * LANES], tb_ref[:, 2 * LANES:3 * LANES]
    for p in range(B_WIDTH // LANES):
        for col, out in ((COL_QB, qb_ref), (COL_KB, kb_ref)):
            xs = z[:, col + p * LANES: col + (p + 1) * LANES]
            r = (xs * c_t + pltpu.roll(xs, HEAD_DIM // 2, 1) * s_hi
                 + pltpu.roll(xs, LANES - HEAD_DIM // 2, 1) * s_lo)
            out[:, p * LANES:(p + 1) * LANES] = r.astype(BF)
    vb_ref[...] = z[:, COL_VB:COL_KPE].astype(BF)


def _proj_even(h, g, w, ta, tb, seq, tm=512):
    T = h.shape[0]
    nseq = seq // tm
    row = lambda c: pl.BlockSpec((tm, c), lambda i: (i, 0))
    tab = lambda c: pl.BlockSpec((tm, c), lambda i: (i % nseq, 0))
    consts = [g, w["win"], w["qg"], w["kvg"], w["wq"], w["wqr"], w["wk"], w["wv"], w["e"]]
    out_cols = (A_PAD, A_PAD, A_PAD, B_WIDTH, B_WIDTH, B_WIDTH)
    return pl.pallas_call(
        _proj_even_kernel,
        grid=(T // tm,),
        in_specs=[row(D_MODEL)] + [_const_spec(c.shape) for c in consts] + [tab(4 * LANES), tab(3 * LANES)],
        out_specs=[row(c) for c in out_cols],
        out_shape=[jax.ShapeDtypeStruct((T, c), BF) for c in out_cols],
        compiler_params=_params(1),
        name="proj_even",
    )(h, *consts, ta, tb)


def _mla_kernel(q_ref, k_ref, v_ref, o_ref):
    for hp in range(A_HEADS // 2):
        acc = None
        for j in range(2):
            sl = slice((2 * hp + j) * LANES, (2 * hp + j + 1) * LANES)
            s = _dot_nt(q_ref[:, sl], k_ref[:, sl])
            m = jnp.max(s, axis=-1, keepdims=True)
            e = jnp.exp(s - m)
            den = jnp.sum(e, axis=-1, keepdims=True)
            o = _dot(e.astype(BF), v_ref[:, sl]) / den
            acc = o if acc is None else acc + o
        o_ref[:, hp * LANES:(hp + 1) * LANES] = acc.astype(BF)


def _mla(qa, ka, va, batch, seq, tq=256):
    q3, k3, v3 = (t.reshape(batch, seq, A_PAD) for t in (qa, ka, va))
    out = pl.pallas_call(
        _mla_kernel,
        grid=(batch, seq // tq),
        in_specs=[pl.BlockSpec((None, tq, A_PAD), lambda b, i: (b, i, 0)),
                  pl.BlockSpec((None, seq, A_PAD), lambda b, i: (b, 0, 0)),
                  pl.BlockSpec((None, seq, A_PAD), lambda b, i: (b, 0, 0))],
        out_specs=pl.BlockSpec((None, tq, B_WIDTH), lambda b, i: (b, i, 0)),
        out_shape=jax.ShapeDtypeStruct((batch, seq, A_HEADS * A_V_DIM), BF),
        compiler_params=_params(2),
        name="mla_attention",
    )(q3, k3, v3)
    return out.reshape(batch * seq, A_HEADS * A_V_DIM)


def _dilated_kernel(q_ref, k_ref, v_ref, o_ref, l_ref, *, n_sub):
    qb = min(128, n_sub)
    kb = min(256, n_sub)
    low = lax.broadcasted_iota(jnp.int32, (qb, LANES), 1) < HEAD_DIM
    low_k = lax.broadcasted_iota(jnp.int32, (kb, LANES), 1) < HEAD_DIM
    rel = (lax.broadcasted_iota(jnp.int32, (qb, kb), 1)
           - lax.broadcasted_iota(jnp.int32, (qb, kb), 0))

    def block(a, carry):
        qs = pl.multiple_of(a * qb, qb)
        ks = pl.multiple_of(jnp.clip(a * qb - HALF_WINDOW, 0, n_sub - kb), HALF_WINDOW)
        valid = jnp.abs(rel + (ks - qs)) <= HALF_WINDOW
        for hp in range(B_WIDTH // LANES):
            sl = slice(hp * LANES, (hp + 1) * LANES)
            q2 = q_ref[pl.ds(qs, qb), sl]
            k2 = k_ref[pl.ds(ks, kb), sl]
            v2 = v_ref[pl.ds(ks, kb), sl]
            outs, lses = [], []
            for j in range(2):
                mq = low if j == 0 else jnp.logical_not(low)
                mk = low_k if j == 0 else jnp.logical_not(low_k)
                s = _dot_nt(jnp.where(mq, q2, jnp.zeros_like(q2)), k2)
                s = jnp.where(valid, s, NEG_INF)
                m = jnp.max(s, axis=-1, keepdims=True)
                e = jnp.exp(s - m)
                den = jnp.sum(e, axis=-1, keepdims=True)
                outs.append(_dot(e.astype(BF), jnp.where(mk, v2, jnp.zeros_like(v2))) / den)
                lses.append(m + jnp.log(den))
            o_ref[pl.ds(qs, qb), sl] = outs[0] + outs[1]
            l_ref[pl.ds(qs, qb), sl] = jnp.where(low, lses[0], lses[1])
        return carry

    lax.fori_loop(0, n_sub // qb, block, 0)


def _dilated_branch(qb, kb, vb, batch, seq, dilation):
    n_sub = seq // dilation
    view = lambda t: t.reshape(batch, n_sub, dilation * B_WIDTH)
    spec = pl.BlockSpec((None, n_sub, B_WIDTH), lambda b, r: (b, 0, r))
    o, lse = pl.pallas_call(
        functools.partial(_dilated_kernel, n_sub=n_sub),
        grid=(batch, dilation),
        in_specs=[spec, spec, spec],
        out_specs=[spec, spec],
        out_shape=[jax.ShapeDtypeStruct((batch, n_sub, dilation * B_WIDTH), F32)] * 2,
        compiler_params=_params(2),
        name=f"dilated_d{dilation}",
    )(view(qb), view(kb), view(vb))
    return o.reshape(batch * seq, B_WIDTH), lse.reshape(batch * seq, B_WIDTH)


def _combine_kernel(o1, o2, o3, l1, l2, l3, out_ref):
    m = jnp.maximum(jnp.maximum(l1[...], l2[...]), l3[...])
    e1, e2, e3 = jnp.exp(l1[...] - m), jnp.exp(l2[...] - m), jnp.exp(l3[...] - m)
    out_ref[...] = ((e1 * o1[...] + e2 * o2[...] + e3 * o3[...]) / (e1 + e2 + e3)).astype(BF)


def _combine(outs, lses, tm=1024):
    T = outs[0].shape[0]
    spec = pl.BlockSpec((tm, B_WIDTH), lambda i: (i, 0))
    return pl.pallas_call(
        _combine_kernel,
        grid=(T // tm,),
        in_specs=[spec] * 6,
        out_specs=spec,
        out_shape=jax.ShapeDtypeStruct((T, B_WIDTH), BF),
        compiler_params=_params(1),
        name="dilated_combine",
    )(*outs, *lses)


def _proj_odd_kernel(h_ref, g_ref, w_ref, o_ref):
    xn = _rms(h_ref[...], g_ref[...]).astype(BF)
    o_ref[...] = _dot(xn, w_ref[...]).astype(BF)


def _proj_odd(h, g, w, tm=512):
    T = h.shape[0]
    n = w.shape[1]
    return pl.pallas_call(
        _proj_odd_kernel,
        grid=(T // tm,),
        in_specs=[pl.BlockSpec((tm, D_MODEL), lambda i: (i, 0)), _const_spec(g.shape), _const_spec(w.shape)],
        out_specs=pl.BlockSpec((tm, n), lambda i: (i, 0)),
        out_shape=jax.ShapeDtypeStruct((T, n), BF),
        compiler_params=_params(1),
        name="proj_odd",
    )(h, g, w)


def _na_kernel(q_ref, k_ref, v_ref, b_ref, o_ref, vm_ref, *, seq):
    rows = seq // GRID_W
    nq = NA_QROWS * GRID_W
    nk = NA_KROWS * GRID_W
    low = lax.broadcasted_iota(jnp.int32, (nq, LANES), 1) < HEAD_DIM
    low_s = lax.broadcasted_iota(jnp.int32, (seq, LANES), 1) < HEAD_DIM
    v_all = v_ref[...]
    vm_ref[0] = jnp.where(low_s, v_all, jnp.zeros_like(v_all))
    vm_ref[1] = jnp.where(low_s, jnp.zeros_like(v_all), v_all)
    n_blocks = rows // NA_QROWS

    def block(a, carry):
        qs = pl.multiple_of(a * nq, nq)
        ks = pl.multiple_of(jnp.clip(NA_QROWS * a - NA_ROWS // 2, 0, rows - NA_KROWS) * GRID_W, GRID_W)
        var = jnp.where(a < 2, a, jnp.where(a >= n_blocks - 2, a - (n_blocks - NA_VARIANTS), 2))
        q2 = q_ref[pl.ds(qs, nq), :]
        k2 = k_ref[pl.ds(ks, nk), :]
        acc = None
        for j in range(2):
            mq = low if j == 0 else jnp.logical_not(low)
            s = _dot_nt(jnp.where(mq, q2, jnp.zeros_like(q2)), k2) + b_ref[j, var]
            m = jnp.max(s, axis=-1, keepdims=True)
            e = jnp.exp(s - m)
            den = jnp.sum(e, axis=-1, keepdims=True)
            o = _dot(e.astype(BF), vm_ref[j, pl.ds(ks, nk), :]) / den
            acc = o if acc is None else acc + o
        o_ref[pl.ds(qs, nq), :] = acc.astype(BF)
        return carry

    lax.fori_loop(0, n_blocks, block, 0)


def _na_bias_table(rpb, seq):
    rows = seq // GRID_W
    n_blocks = rows // NA_QROWS
    col = np.arange(GRID_W)
    cs = np.clip(col - NA_COLS // 2, 0, GRID_W - NA_COLS)
    col_valid = (col[None, :] >= cs[:, None]) & (col[None, :] < cs[:, None] + NA_COLS)
    col_off = np.clip(col[None, :] - col[:, None] + NA_COLS - 1, 0, 2 * NA_COLS - 2)
    t2 = jnp.where(col_valid[None, None], rpb[:, :, col_off], NEG_INF)
    t2 = jnp.concatenate([t2, jnp.full_like(t2[:, :1], NEG_INF)], axis=1)
    n_off = 2 * NA_ROWS - 1
    idx = np.full((NA_VARIANTS, NA_QROWS, NA_KROWS), n_off, np.int32)
    for v, a in enumerate((0, 1, 2, n_blocks - 2, n_blocks - 1)):
        ws = int(np.clip(NA_QROWS * a - NA_ROWS // 2, 0, rows - NA_KROWS))
        for qr in range(NA_QROWS):
            r = NA_QROWS * a + qr
            rs = int(np.clip(r - NA_ROWS // 2, 0, rows - NA_ROWS))
            for kr in range(NA_KROWS):
                krow = ws + kr
                if rs <= krow < rs + NA_ROWS:
                    idx[v, qr, kr] = krow - r + NA_ROWS - 1
    big = t2[:, idx]
    big = big.transpose(0, 1, 2, 4, 3, 5)
    return big.reshape(C_HEADS, NA_VARIANTS, NA_QROWS * GRID_W, NA_KROWS * GRID_W)


def _neighbourhood(qkv, bias, batch, seq):
    qkv3 = qkv.reshape(batch, seq, 3 * C_WIDTH)
    n_hp = C_WIDTH // LANES
    blk = lambda off: pl.BlockSpec((None, seq, LANES), lambda hp, b: (b, 0, off + hp))
    out = pl.pallas_call(
        functools.partial(_na_kernel, seq=seq),
        grid=(n_hp, batch),
        in_specs=[blk(0), blk(n_hp), blk(2 * n_hp),
                  pl.BlockSpec((2,) + bias.shape[1:], lambda hp, b: (hp, 0, 0, 0))],
        out_specs=pl.BlockSpec((None, seq, LANES), lambda hp, b: (b, 0, hp)),
        out_shape=jax.ShapeDtypeStruct((batch, seq, C_WIDTH), BF),
        scratch_shapes=[pltpu.VMEM((2, seq, LANES), BF)],
        compiler_params=_params(2),
        name="neighbourhood_attention",
    )(qkv3, qkv3, qkv3, bias)
    return out.reshape(batch * seq, C_WIDTH)


def _post_kernel(*refs, n_attn, final):
    h_ref = refs[0]
    attn = refs[1:1 + n_attn]
    wo_ref, g_ref, wg_ref, wu_ref, wd_ref = refs[1 + n_attn:6 + n_attn]
    fg_ref = refs[6 + n_attn] if final else None
    out_ref = refs[-1]
    h1 = h_ref[...]
    off = 0
    for a in attn:
        width = a.shape[1]
        h1 = h1 + _dot(a[...], wo_ref[off:off + width, :])
        off += width
    xn = _rms(h1, g_ref[...]).astype(BF)
    gate = _dot(xn, wg_ref[...])
    up = _dot(xn, wu_ref[...])
    act = (gate * (1.0 / (1.0 + jnp.exp(-gate))) * up).astype(BF)
    y = h1 + _dot(act, wd_ref[...])
    if final:
        y = _rms(y, fg_ref[...])
    out_ref[...] = y


def _post(h, attn, wo, g, wg, wu, wd, final_g=None, tm=512):
    T = h.shape[0]
    final = final_g is not None
    row = lambda c: pl.BlockSpec((tm, c), lambda i: (i, 0))
    consts = [wo, g, wg, wu, wd] + ([final_g] if final else [])
    return pl.pallas_call(
        functools.partial(_post_kernel, n_attn=len(attn), final=final),
        grid=(T // tm,),
        in_specs=[row(D_MODEL)] + [row(a.shape[1]) for a in attn] + [_const_spec(c.shape) for c in consts],
        out_specs=row(D_MODEL),
        out_shape=jax.ShapeDtypeStruct((T, D_MODEL), F32),
        compiler_params=_params(1),
        name="outproj_ffn",
    )(h, *attn, *consts)


def _rope_tables(seq):
    pos = jnp.arange(seq, dtype=F32)

    def cs(dim):
        inv = ROPE_THETA ** (-jnp.arange(0, dim, 2, dtype=F32) / dim)
        ang = pos[:, None] * inv[None, :]
        return jnp.cos(ang), jnp.sin(ang)

    cos_a, sin_a = cs(A_ROPE_DIM)
    cos2, sin2 = jnp.tile(cos_a, (1, 2)), jnp.tile(sin_a, (1, 2))
    zeros = lambda n: jnp.zeros((seq, n), F32)
    scale = (A_NOPE_DIM + A_ROPE_DIM) ** -0.5
    pad = LANES - A_NOPE_DIM - A_ROPE_DIM
    cq = jnp.concatenate([jnp.full((seq, A_NOPE_DIM), scale, F32), cos2 * scale, zeros(pad)], axis=1)
    sq = jnp.concatenate([zeros(A_NOPE_DIM), sin2 * scale, zeros(pad)], axis=1)
    ck = jnp.concatenate([cos2, zeros(LANES - A_ROPE_DIM)], axis=1)
    sk = jnp.concatenate([sin2, zeros(LANES - A_ROPE_DIM)], axis=1)
    ta = jnp.concatenate([cq, sq, ck, sk], axis=1)
    cos_b, sin_b = cs(HEAD_DIM)
    zb = jnp.zeros_like(sin_b)
    c = jnp.tile(cos_b, (1, 4))
    s_hi = jnp.tile(jnp.concatenate([zb, sin_b], axis=1), (1, 2))
    s_lo = jnp.tile(jnp.concatenate([-sin_b, zb], axis=1), (1, 2))
    tb = jnp.concatenate([c, s_hi, s_lo], axis=1)
    return ta, tb


def _rot_half_cols(w):
    half = w.shape[1] // 2
    return jnp.concatenate([-w[:, half:], w[:, :half]], axis=1)


def _prep_even(w_in, q_norm, w_q_up, kv_norm, w_kv_up, w_out):
    d = w_in.shape[0]
    c_q, c_kv, k_pe, qkv_b = jnp.split(w_in, [A_Q_RANK, COL_QB, COL_QB + A_ROPE_DIM], axis=1)
    q_b, k_b, v_b = jnp.split(qkv_b, 3, axis=1)
    padc = jnp.zeros((d, LANES - A_ROPE_DIM), F32)
    win = jnp.concatenate([c_q, c_kv, q_b * HEAD_DIM ** -0.5, k_b, v_b,
                           k_pe, padc, _rot_half_cols(k_pe), padc], axis=1).astype(BF)
    wq3 = w_q_up.reshape(A_Q_RANK, A_HEADS, A_NOPE_DIM + A_ROPE_DIM)
    nope, rope = wq3[..., :A_NOPE_DIM], wq3[..., A_NOPE_DIM:]
    zpad = jnp.zeros((A_Q_RANK, A_HEADS, LANES - A_NOPE_DIM - A_ROPE_DIM), F32)
    wq = jnp.concatenate([nope, rope, zpad], axis=-1).reshape(A_Q_RANK, A_PAD).astype(BF)
    rope_rot = jnp.concatenate([-rope[..., A_ROPE_DIM // 2:], rope[..., :A_ROPE_DIM // 2]], axis=-1)
    wqr = jnp.concatenate([jnp.zeros_like(nope), rope_rot, zpad], axis=-1).reshape(A_Q_RANK, A_PAD).astype(BF)
    wkv3 = w_kv_up.reshape(A_KV_RANK, A_HEADS, A_NOPE_DIM + A_V_DIM)
    k_nope, v = wkv3[..., :A_NOPE_DIM], wkv3[..., A_NOPE_DIM:]
    zk = jnp.zeros_like(k_nope)
    wk = jnp.concatenate([k_nope, zk], axis=-1).reshape(A_KV_RANK, A_PAD).astype(BF)
    v_even = jnp.concatenate([v, jnp.zeros_like(v)], axis=-1)
    v_odd = jnp.concatenate([jnp.zeros_like(v), v], axis=-1)
    odd = (np.arange(A_HEADS) % 2 == 1)[None, :, None]
    wv = jnp.where(odd, v_odd, v_even).reshape(A_KV_RANK, A_PAD).astype(BF)
    e = np.zeros((LANES, A_PAD), np.float32)
    for hd in range(A_HEADS):
        for j in range(A_ROPE_DIM):
            e[j, hd * LANES + A_NOPE_DIM + j] = 1.0
    return dict(win=win, qg=q_norm[None, :], kvg=kv_norm[None, :], wq=wq, wqr=wqr, wk=wk, wv=wv,
                e=jnp.asarray(e, BF), wo=w_out.astype(BF))


def kernel(x, ev_norm, ev_w_in, ev_q_norm, ev_w_q_up, ev_kv_norm, ev_w_kv_up, ev_w_out, od_norm, od_w_qkv, od_rpb, od_w_out, ffn_norm, ffn_w_gate, ffn_w_up, ffn_w_down, final_norm):
    batch, seq, d = x.shape
    h = x.reshape(batch * seq, d)
    ta, tb = _rope_tables(seq)
    for layer in range(DEPTH):
        i = layer // 2
        if layer % 2 == 0:
            w = _prep_even(ev_w_in[i], ev_q_norm[i], ev_w_q_up[i], ev_kv_norm[i], ev_w_kv_up[i], ev_w_out[i])
            qa, ka, va, qb, kb, vb = _proj_even(h, ev_norm[i][None, :], w, ta, tb, seq)
            o_a = _mla(qa, ka, va, batch, seq)
            branches = [_dilated_branch(qb, kb, vb, batch, seq, dil) for _, dil in B_BRANCHES]
            o_b = _combine([o for o, _ in branches], [l for _, l in branches])
            attn, wo = [o_a, o_b], w["wo"]
        else:
            wqkv = jnp.concatenate([od_w_qkv[i][:, :C_WIDTH] * HEAD_DIM ** -0.5, od_w_qkv[i][:, C_WIDTH:]],
                                   axis=1).astype(BF)
            qkv = _proj_odd(h, od_norm[i][None, :], wqkv)
            attn = [_neighbourhood(qkv, _na_bias_table(od_rpb[i], seq), batch, seq)]
            wo = od_w_out[i].astype(BF)
        h = _post(h, attn, wo, ffn_norm[layer][None, :], ffn_w_gate[layer].astype(BF),
                  ffn_w_up[layer].astype(BF), ffn_w_down[layer].astype(BF),
                  final_g=final_norm[None, :] if layer == DEPTH - 1 else None)
    return h.reshape(batch, seq, d)
```

```python
import functools

import numpy as np
import jax
import jax.numpy as jnp
from jax import lax
from jax.experimental import pallas as pl
from jax.experimental.pallas import tpu as pltpu

BF = jnp.bfloat16
F32 = jnp.float32

D_MODEL = 1024
DEPTH = 4
HEAD_DIM = 64
A_HEADS = 8
A_Q_RANK = 256
A_KV_RANK = 128
A_NOPE_DIM = 64
A_ROPE_DIM = 32
A_V_DIM = 64
B_HEADS = 8
B_BRANCHES = ((128, 1), (512, 4), (2048, 16))
C_HEADS = 16
GRID_W = 64
NA_ROWS = 8
NA_COLS = 16
D_FF = -(-8 * D_MODEL // (3 * 256)) * 256
ROPE_THETA = 10000.0
EPS = 1e-6
NEG_INF = -1e30

LANES = 128
B_WIDTH = B_HEADS * HEAD_DIM
C_WIDTH = C_HEADS * HEAD_DIM
A_PAD = A_HEADS * LANES
IN_COLS = A_Q_RANK + A_KV_RANK + 3 * B_WIDTH + 2 * LANES
COL_QB = A_Q_RANK + A_KV_RANK
COL_KB = COL_QB + B_WIDTH
COL_VB = COL_KB + B_WIDTH
COL_KPE = COL_VB + B_WIDTH
COL_KPE_ROT = COL_KPE + LANES
HALF_WINDOW = 64
NA_QROWS = 2
NA_KROWS = 10
NA_VARIANTS = 5
NA_UNROLL = 4
RES_GROUPS = 16
DIL_QB = 128
DIL_KB = 256
DIL_UNROLL = 4
DEN_LANE = (HEAD_DIM, 0)
VMEM_LIMIT = 56 * 1024 * 1024


def _rms(x, g):
    return x * lax.rsqrt(jnp.mean(x * x, axis=-1, keepdims=True) + EPS) * g


def _dot(a, b):
    return jnp.dot(a, b, preferred_element_type=F32)


def _dot_nt(a, b):
    return lax.dot_general(a, b, (((1,), (1,)), ((), ())), preferred_element_type=F32)


def _const_spec(shape):
    nd = len(shape)
    return pl.BlockSpec(shape, lambda *_: (0,) * nd, pipeline_mode=pl.Buffered(1))


def _params(n_grid):
    return pltpu.CompilerParams(dimension_semantics=("arbitrary",) * n_grid,
                                vmem_limit_bytes=VMEM_LIMIT)


def _proj_even_kernel(h_ref, g_ref, win_ref, qg_ref, kvg_ref, wq_ref, wqr_ref, wk_ref, wv_ref,
                      e_ref, ta_ref, tb_ref, qa_ref, ka_ref, va_ref, qb_ref, kb_ref, vb_ref,
                      qr_ref, kr_ref, vr_ref, stage_ref):
    xn = _rms(h_ref[...], g_ref[...]).astype(BF)
    z = _dot(xn, win_ref[...])
    cq = _rms(z[:, 0:A_Q_RANK], qg_ref[...]).astype(BF)
    ckv = _rms(z[:, A_Q_RANK:COL_QB], kvg_ref[...]).astype(BF)
    qa = _dot(cq, wq_ref[...])
    qr = _dot(cq, wqr_ref[...])
    cq_t, sq_t = ta_ref[:, 0:LANES], ta_ref[:, LANES:2 * LANES]
    ck_t, sk_t = ta_ref[:, 2 * LANES:3 * LANES], ta_ref[:, 3 * LANES:4 * LANES]
    for hd in range(A_HEADS):
        sl = slice(hd * LANES, (hd + 1) * LANES)
        qa_ref[:, sl] = (qa[:, sl] * cq_t + qr[:, sl] * sq_t).astype(BF)
    kpe = z[:, COL_KPE:COL_KPE + LANES] * ck_t + z[:, COL_KPE_ROT:COL_KPE_ROT + LANES] * sk_t
    ka_ref[...] = (_dot(ckv, wk_ref[...]) + _dot(kpe.astype(BF), e_ref[...])).astype(BF)
    va_ref[...] = _dot(ckv, wv_ref[...]).astype(BF)
    c_t, s_hi, s_lo = tb_ref[:, 0:LANES], tb_ref[:, LANES:2 * LANES], tb_ref[:, 2 * LANES:3 * LANES]
    for p in range(B_WIDTH // LANES):
        for idx, (col, out) in enumerate(((COL_QB, qb_ref), (COL_KB, kb_ref))):
            xs = z[:, col + p * LANES: col + (p + 1) * LANES]
            r = (xs * c_t + pltpu.roll(xs, HEAD_DIM // 2, 1) * s_hi
                 + pltpu.roll(xs, LANES - HEAD_DIM // 2, 1) * s_lo)
            out[:, p * LANES:(p + 1) * LANES] = r.astype(BF)
            stage_ref[idx, p] = r
        stage_ref[2, p] = z[:, COL_VB + p * LANES:COL_VB + (p + 1) * LANES]
    vb_ref[...] = z[:, COL_VB:COL_KPE].astype(BF)
    rows = stage_ref.shape[2] // RES_GROUPS
    for idx, out in enumerate((qr_ref, kr_ref, vr_ref)):
        for p in range(B_WIDTH // LANES):
            for r in range(RES_GROUPS):
                out[r, :, p * LANES:(p + 1) * LANES] = (
                    stage_ref[idx, p, pl.ds(r, rows, stride=RES_GROUPS), :].astype(BF))


def _proj_even(h, g, w, ta, tb, seq, tm=512):
    T = h.shape[0]
    nseq = seq // tm
    row = lambda c: pl.BlockSpec((tm, c), lambda i: (i, 0))
    tab = lambda c: pl.BlockSpec((tm, c), lambda i: (i % nseq, 0))
    consts = [g, w["win"], w["qg"], w["kvg"], w["wq"], w["wqr"], w["wk"], w["wv"], w["e"]]
    out_cols = (A_PAD, A_PAD, A_PAD, B_WIDTH, B_WIDTH, B_WIDTH)
    grouped = pl.BlockSpec((None, RES_GROUPS, tm // RES_GROUPS, B_WIDTH), lambda i: (i // nseq, 0, i % nseq, 0))
    grouped_shape = jax.ShapeDtypeStruct((T // seq, RES_GROUPS, seq // RES_GROUPS, B_WIDTH), BF)
    return pl.pallas_call(
        _proj_even_kernel,
        grid=(T // tm,),
        in_specs=[row(D_MODEL)] + [_const_spec(c.shape) for c in consts] + [tab(4 * LANES), tab(3 * LANES)],
        out_specs=[row(c) for c in out_cols] + [grouped] * 3,
        out_shape=[jax.ShapeDtypeStruct((T, c), BF) for c in out_cols] + [grouped_shape] * 3,
        scratch_shapes=[pltpu.VMEM((3, B_WIDTH // LANES, tm, LANES), F32)],
        compiler_params=_params(1),
        name="proj_even",
    )(h, *consts, ta, tb)


def _mla_kernel(q_ref, k_ref, v_ref, o_ref):
    for hp in range(A_HEADS // 2):
        acc = None
        for j in range(2):
            sl = slice((2 * hp + j) * LANES, (2 * hp + j + 1) * LANES)
            s = _dot_nt(q_ref[:, sl], k_ref[:, sl])
            m = jnp.max(s, axis=-1, keepdims=True)
            e = jnp.exp(s - m)
            den = jnp.sum(e, axis=-1, keepdims=True)
            o = _dot(e.astype(BF), v_ref[:, sl]) / den
            acc = o if acc is None else acc + o
        o_ref[:, hp * LANES:(hp + 1) * LANES] = acc.astype(BF)


def _mla(qa, ka, va, batch, seq, tq=256):
    q3, k3, v3 = (t.reshape(batch, seq, A_PAD) for t in (qa, ka, va))
    out = pl.pallas_call(
        _mla_kernel,
        grid=(batch, seq // tq),
        in_specs=[pl.BlockSpec((None, tq, A_PAD), lambda b, i: (b, i, 0)),
                  pl.BlockSpec((None, seq, A_PAD), lambda b, i: (b, 0, 0)),
                  pl.BlockSpec((None, seq, A_PAD), lambda b, i: (b, 0, 0))],
        out_specs=pl.BlockSpec((None, tq, B_WIDTH), lambda b, i: (b, i, 0)),
        out_shape=jax.ShapeDtypeStruct((batch, seq, A_HEADS * A_V_DIM), BF),
        compiler_params=_params(2),
        name="mla_attention",
    )(q3, k3, v3)
    return out.reshape(batch * seq, A_HEADS * A_V_DIM)


def _head_masks(n):
    lane = lax.broadcasted_iota(jnp.int32, (n, LANES), 1)
    low = lane < HEAD_DIM
    ones = [(lane == DEN_LANE[j]).astype(BF) for j in range(2)]
    return low, ones


def _pair_scores(q2, k2, bias, low_q):
    zero = jnp.zeros_like(q2)
    return [_dot_nt(jnp.where(low_q, q2, zero), k2) + bias, _dot_nt(jnp.where(low_q, zero, q2), k2) + bias]


def _pair_values(v2, low_k, ones_k):
    return [jnp.where(low_k, v2, ones_k[0]), jnp.where(low_k, ones_k[1], v2)]


def _exp_rows(s):
    m = jnp.max(s, axis=-1, keepdims=True)
    return jnp.exp(s - m).astype(BF), m


def _pair_finish(o0, o1, low_q, m0=None, m1=None):
    den0 = o0[:, DEN_LANE[0]:DEN_LANE[0] + 1]
    den1 = o1[:, DEN_LANE[1]:DEN_LANE[1] + 1]
    out = jnp.where(low_q, o0 / den0, o1 / den1)
    if m0 is None:
        return out
    return out, jnp.where(low_q, m0 + jnp.log(den0), m1 + jnp.log(den1))


def _attend_blocks(blocks, low_q, low_k, ones_k):
    scores = [_pair_scores(q2, k2, bias, low_q) for q2, k2, _, bias in blocks]
    probs = [[_exp_rows(s) for s in pair] for pair in scores]
    results = []
    for (_, _, v2, _), pair in zip(blocks, probs):
        vals = _pair_values(v2, low_k, ones_k)
        o0, o1 = _dot(pair[0][0], vals[0]), _dot(pair[1][0], vals[1])
        results.append(_pair_finish(o0, o1, low_q, pair[0][1], pair[1][1]))
    return results


def _grouped_branch(qg_ref, kg_ref, vg_ref, mask_ref, mask_base, o_ref, l_ref, dil):
    n16 = qg_ref.shape[1]
    per, jq, jk, n_jb = _grouped_geometry(dil, n16)
    nq, nk = per * jq, per * jk
    low_q, _ = _head_masks(nq)
    low_k, ones_k = _head_masks(nk)
    r_unroll = max(1, DIL_UNROLL // n_jb)

    def body(i, carry):
        blocks, where = [], []
        for ru in range(r_unroll):
            rr = i * r_unroll + ru
            for jb in range(n_jb):
                q0, k0 = _grouped_window(jb, per, jq, jk, n16)
                cat = lambda ref, lo, n: jnp.concatenate(
                    [ref[rr + dil * u, lo:lo + n, :] for u in range(per)], axis=0)
                blocks.append((cat(qg_ref, q0, jq), cat(kg_ref, k0, jk), cat(vg_ref, k0, jk),
                               mask_ref[mask_base + jb, :, 0:nk]))
                where.append((rr, q0))
        for (rr, q0), (o, l) in zip(where, _attend_blocks(blocks, low_q, low_k, ones_k)):
            for u in range(per):
                o_ref[rr + dil * u, q0:q0 + jq, :] = o[u * jq:(u + 1) * jq]
                l_ref[rr + dil * u, q0:q0 + jq, :] = l[u * jq:(u + 1) * jq]
        return carry

    lax.fori_loop(0, dil // r_unroll, body, 0)


def _grouped_geometry(dil, n16):
    per = RES_GROUPS // dil
    jq = DIL_QB // per
    jk = min(DIL_KB, n16 * per) // per
    return per, jq, jk, n16 // jq


def _grouped_window(jb, per, jq, jk, n16):
    q0 = jq * jb
    return q0, int(np.clip(q0 - HALF_WINDOW // per, 0, n16 - jk))


def _dilated_masks(seq):
    n16 = seq // RES_GROUPS
    neg = np.float32(NEG_INF)
    q = np.arange(DIL_QB)[:, None]
    k = np.arange(DIL_KB)[None, :]
    tiles = []
    n_blocks = seq // DIL_QB
    for a in (0, 1, n_blocks - 1):
        qs = a * DIL_QB
        ks = int(np.clip(qs - HALF_WINDOW, 0, seq - DIL_KB))
        tiles.append(np.where(np.abs((k + ks) - (q + qs)) <= HALF_WINDOW, 0, neg))
    for _, dil in B_BRANCHES[1:]:
        per, jq, jk, n_jb = _grouped_geometry(dil, n16)
        pos = lambda i, n: per * (i % n) + i // n
        for jb in range(n_jb):
            q0, k0 = _grouped_window(jb, per, jq, jk, n16)
            rel = (pos(k, jk) + per * k0) - (pos(q, jq) + per * q0)
            tile = np.where(np.abs(rel) <= HALF_WINDOW, 0, neg)
            tile[:, per * jk:] = neg
            tiles.append(tile)
    return jnp.asarray(np.stack(tiles).astype(np.float32))


def _dilated_kernel(qn_ref, kn_ref, vn_ref, qg_ref, kg_ref, vg_ref, mask_ref, out_ref,
                    o1_ref, l1_ref, o2_ref, l2_ref, o3_ref, l3_ref, mix_ref):
    seq = qn_ref.shape[0]
    n16 = seq // RES_GROUPS
    n_blocks = seq // DIL_QB
    low_q, _ = _head_masks(DIL_QB)
    low_k, ones_k = _head_masks(DIL_KB)

    def body(i, carry):
        blocks, starts = [], []
        for u in range(DIL_UNROLL):
            a = i * DIL_UNROLL + u
            qs = pl.multiple_of(a * DIL_QB, DIL_QB)
            ks = pl.multiple_of(jnp.clip(a * DIL_QB - HALF_WINDOW, 0, seq - DIL_KB), HALF_WINDOW)
            variant = jnp.where(a == 0, 0, jnp.where(a == n_blocks - 1, 2, 1))
            blocks.append((qn_ref[pl.ds(qs, DIL_QB), :], kn_ref[pl.ds(ks, DIL_KB), :],
                           vn_ref[pl.ds(ks, DIL_KB), :], mask_ref[variant]))
            starts.append(qs)
        for qs, (o, l) in zip(starts, _attend_blocks(blocks, low_q, low_k, ones_k)):
            o1_ref[pl.ds(qs, DIL_QB), :] = o
            l1_ref[pl.ds(qs, DIL_QB), :] = l
        return carry

    lax.fori_loop(0, n_blocks // DIL_UNROLL, body, 0)
    base4 = 3
    base16 = base4 + _grouped_geometry(B_BRANCHES[1][1], n16)[3]
    _grouped_branch(qg_ref, kg_ref, vg_ref, mask_ref, base4, o2_ref, l2_ref, B_BRANCHES[1][1])
    _grouped_branch(qg_ref, kg_ref, vg_ref, mask_ref, base16, o3_ref, l3_ref, B_BRANCHES[2][1])
    for r in range(RES_GROUPS):
        rows = pl.ds(r, n16, stride=RES_GROUPS)
        la, lb, lc = l1_ref[rows, :], l2_ref[r], l3_ref[r]
        m = jnp.maximum(jnp.maximum(la, lb), lc)
        ea, eb, ec = jnp.exp(la - m), jnp.exp(lb - m), jnp.exp(lc - m)
        mix_ref[rows, :] = (ea * o1_ref[rows, :] + eb * o2_ref[r] + ec * o3_ref[r]) / (ea + eb + ec)
    out_ref[...] = mix_ref[...].astype(BF)


def _dilated(qn, kn, vn, qg, kg, vg, batch, seq):
    n16 = seq // RES_GROUPS
    nat = pl.BlockSpec((None, seq, LANES), lambda b, hp: (b, 0, hp))
    grp = pl.BlockSpec((None, RES_GROUPS, n16, LANES), lambda b, hp: (b, 0, 0, hp))
    nat3 = lambda t: t.reshape(batch, seq, B_WIDTH)
    masks = _dilated_masks(seq)
    out = pl.pallas_call(
        _dilated_kernel,
        grid=(batch, B_WIDTH // LANES),
        in_specs=[nat, nat, nat, grp, grp, grp, _const_spec(masks.shape)],
        out_specs=nat,
        out_shape=jax.ShapeDtypeStruct((batch, seq, B_WIDTH), BF),
        scratch_shapes=[pltpu.VMEM((seq, LANES), F32), pltpu.VMEM((seq, LANES), F32)]
        + [pltpu.VMEM((RES_GROUPS, n16, LANES), F32)] * 4 + [pltpu.VMEM((seq, LANES), F32)],
        compiler_params=_params(2),
        name="dilated_attention",
    )(nat3(qn), nat3(kn), nat3(vn), qg, kg, vg, masks)
    return out.reshape(batch * seq, B_WIDTH)


def _proj_odd_kernel(h_ref, g_ref, w_ref, o_ref):
    xn = _rms(h_ref[...], g_ref[...]).astype(BF)
    o_ref[...] = _dot(xn, w_ref[...]).astype(BF)


def _proj_odd(h, g, w, tm=512):
    T = h.shape[0]
    n = w.shape[1]
    return pl.pallas_call(
        _proj_odd_kernel,
        grid=(T // tm,),
        in_specs=[pl.BlockSpec((tm, D_MODEL), lambda i: (i, 0)), _const_spec(g.shape), _const_spec(w.shape)],
        out_specs=pl.BlockSpec((tm, n), lambda i: (i, 0)),
        out_shape=jax.ShapeDtypeStruct((T, n), BF),
        compiler_params=_params(1),
        name="proj_odd",
    )(h, g, w)


def _na_kernel(q_ref, k_ref, v_ref, b_ref, o_ref, vm_ref, *, seq):
    rows = seq // GRID_W
    nq = NA_QROWS * GRID_W
    nk = NA_KROWS * GRID_W
    low, _ = _head_masks(nq)
    low_s, ones_s = _head_masks(seq)
    vals = _pair_values(v_ref[...], low_s, ones_s)
    vm_ref[0] = vals[0]
    vm_ref[1] = vals[1]
    n_blocks = rows // NA_QROWS

    def body(i, carry):
        scores, where = [], []
        for u in range(NA_UNROLL):
            a = i * NA_UNROLL + u
            qs = pl.multiple_of(a * nq, nq)
            ks = pl.multiple_of(jnp.clip(NA_QROWS * a - NA_ROWS // 2, 0, rows - NA_KROWS) * GRID_W, GRID_W)
            var = jnp.where(a < 2, a, jnp.where(a >= n_blocks - 2, a - (n_blocks - NA_VARIANTS), 2))
            q2 = q_ref[pl.ds(qs, nq), :]
            k2 = k_ref[pl.ds(ks, nk), :]
            zero = jnp.zeros_like(q2)
            scores.append([_dot_nt(jnp.where(low, q2, zero), k2) + b_ref[0, var],
                           _dot_nt(jnp.where(low, zero, q2), k2) + b_ref[1, var]])
            where.append((qs, ks))
        probs = [[_exp_rows(s)[0] for s in pair] for pair in scores]
        for (qs, ks), pair in zip(where, probs):
            o0 = _dot(pair[0], vm_ref[0, pl.ds(ks, nk), :])
            o1 = _dot(pair[1], vm_ref[1, pl.ds(ks, nk), :])
            o_ref[pl.ds(qs, nq), :] = _pair_finish(o0, o1, low).astype(BF)
        return carry

    lax.fori_loop(0, n_blocks // NA_UNROLL, body, 0)


def _na_bias_table(rpb, seq):
    rows = seq // GRID_W
    n_blocks = rows // NA_QROWS
    col = np.arange(GRID_W)
    cs = np.clip(col - NA_COLS // 2, 0, GRID_W - NA_COLS)
    col_valid = (col[None, :] >= cs[:, None]) & (col[None, :] < cs[:, None] + NA_COLS)
    col_off = np.clip(col[None, :] - col[:, None] + NA_COLS - 1, 0, 2 * NA_COLS - 2)
    t2 = jnp.where(col_valid[None, None], rpb[:, :, col_off], NEG_INF)
    t2 = jnp.concatenate([t2, jnp.full_like(t2[:, :1], NEG_INF)], axis=1)
    n_off = 2 * NA_ROWS - 1
    idx = np.full((NA_VARIANTS, NA_QROWS, NA_KROWS), n_off, np.int32)
    for v, a in enumerate((0, 1, 2, n_blocks - 2, n_blocks - 1)):
        ws = int(np.clip(NA_QROWS * a - NA_ROWS // 2, 0, rows - NA_KROWS))
        for qr in range(NA_QROWS):
            r = NA_QROWS * a + qr
            rs = int(np.clip(r - NA_ROWS // 2, 0, rows - NA_ROWS))
            for kr in range(NA_KROWS):
                krow = ws + kr
                if rs <= krow < rs + NA_ROWS:
                    idx[v, qr, kr] = krow - r + NA_ROWS - 1
    big = t2[:, idx]
    big = big.transpose(0, 1, 2, 4, 3, 5)
    return big.reshape(C_HEADS, NA_VARIANTS, NA_QROWS * GRID_W, NA_KROWS * GRID_W)


def _neighbourhood(qkv, bias, batch, seq):
    qkv3 = qkv.reshape(batch, seq, 3 * C_WIDTH)
    n_hp = C_WIDTH // LANES
    blk = lambda off: pl.BlockSpec((None, seq, LANES), lambda hp, b: (b, 0, off + hp))
    out = pl.pallas_call(
        functools.partial(_na_kernel, seq=seq),
        grid=(n_hp, batch),
        in_specs=[blk(0), blk(n_hp), blk(2 * n_hp),
                  pl.BlockSpec((2,) + bias.shape[1:], lambda hp, b: (hp, 0, 0, 0))],
        out_specs=pl.BlockSpec((None, seq, LANES), lambda hp, b: (b, 0, hp)),
        out_shape=jax.ShapeDtypeStruct((batch, seq, C_WIDTH), BF),
        scratch_shapes=[pltpu.VMEM((2, seq, LANES), BF)],
        compiler_params=_params(2),
        name="neighbourhood_attention",
    )(qkv3, qkv3, qkv3, bias)
    return out.reshape(batch * seq, C_WIDTH)


def _post_kernel(*refs, n_attn, final):
    h_ref = refs[0]
    attn = refs[1:1 + n_attn]
    wo_ref, g_ref, wg_ref, wu_ref, wd_ref = refs[1 + n_attn:6 + n_attn]
    fg_ref = refs[6 + n_attn] if final else None
    out_ref = refs[-1]
    h1 = h_ref[...]
    off = 0
    for a in attn:
        width = a.shape[1]
        h1 = h1 + _dot(a[...], wo_ref[off:off + width, :])
        off += width
    xn = _rms(h1, g_ref[...]).astype(BF)
    gate = _dot(xn, wg_ref[...])
    up = _dot(xn, wu_ref[...])
    act = (gate * (1.0 / (1.0 + jnp.exp(-gate))) * up).astype(BF)
    y = h1 + _dot(act, wd_ref[...])
    if final:
        y = _rms(y, fg_ref[...])
    out_ref[...] = y


def _post(h, attn, wo, g, wg, wu, wd, final_g=None, tm=512):
    T = h.shape[0]
    final = final_g is not None
    row = lambda c: pl.BlockSpec((tm, c), lambda i: (i, 0))
    consts = [wo, g, wg, wu, wd] + ([final_g] if final else [])
    return pl.pallas_call(
        functools.partial(_post_kernel, n_attn=len(attn), final=final),
        grid=(T // tm,),
        in_specs=[row(D_MODEL)] + [row(a.shape[1]) for a in attn] + [_const_spec(c.shape) for c in consts],
        out_specs=row(D_MODEL),
        out_shape=jax.ShapeDtypeStruct((T, D_MODEL), F32),
        compiler_params=_params(1),
        name="outproj_ffn",
    )(h, *attn, *consts)


def _rope_tables(seq):
    pos = jnp.arange(seq, dtype=F32)

    def cs(dim):
        inv = ROPE_THETA ** (-jnp.arange(0, dim, 2, dtype=F32) / dim)
        ang = pos[:, None] * inv[None, :]
        return jnp.cos(ang), jnp.sin(ang)

    cos_a, sin_a = cs(A_ROPE_DIM)
    cos2, sin2 = jnp.tile(cos_a, (1, 2)), jnp.tile(sin_a, (1, 2))
    zeros = lambda n: jnp.zeros((seq, n), F32)
    scale = (A_NOPE_DIM + A_ROPE_DIM) ** -0.5
    pad = LANES - A_NOPE_DIM - A_ROPE_DIM
    cq = jnp.concatenate([jnp.full((seq, A_NOPE_DIM), scale, F32), cos2 * scale, zeros(pad)], axis=1)
    sq = jnp.concatenate([zeros(A_NOPE_DIM), sin2 * scale, zeros(pad)], axis=1)
    ck = jnp.concatenate([cos2, zeros(LANES - A_ROPE_DIM)], axis=1)
    sk = jnp.concatenate([sin2, zeros(LANES - A_ROPE_DIM)], axis=1)
    ta = jnp.concatenate([cq, sq, ck, sk], axis=1)
    cos_b, sin_b = cs(HEAD_DIM)
    zb = jnp.zeros_like(sin_b)
    c = jnp.tile(cos_b, (1, 4))
    s_hi = jnp.tile(jnp.concatenate([zb, sin_b], axis=1), (1, 2))
    s_lo = jnp.tile(jnp.concatenate([-sin_b, zb], axis=1), (1, 2))
    tb = jnp.concatenate([c, s_hi, s_lo], axis=1)
    return ta, tb


def _rot_half_cols(w):
    half = w.shape[1] // 2
    return jnp.concatenate([-w[:, half:], w[:, :half]], axis=1)


def _prep_even(w_in, q_norm, w_q_up, kv_norm, w_kv_up, w_out):
    d = w_in.shape[0]
    c_q, c_kv, k_pe, qkv_b = jnp.split(w_in, [A_Q_RANK, COL_QB, COL_QB + A_ROPE_DIM], axis=1)
    q_b, k_b, v_b = jnp.split(qkv_b, 3, axis=1)
    padc = jnp.zeros((d, LANES - A_ROPE_DIM), F32)
    win = jnp.concatenate([c_q, c_kv, q_b * HEAD_DIM ** -0.5, k_b, v_b,
                           k_pe, padc, _rot_half_cols(k_pe), padc], axis=1).astype(BF)
    wq3 = w_q_up.reshape(A_Q_RANK, A_HEADS, A_NOPE_DIM + A_ROPE_DIM)
    nope, rope = wq3[..., :A_NOPE_DIM], wq3[..., A_NOPE_DIM:]
    zpad = jnp.zeros((A_Q_RANK, A_HEADS, LANES - A_NOPE_DIM - A_ROPE_DIM), F32)
    wq = jnp.concatenate([nope, rope, zpad], axis=-1).reshape(A_Q_RANK, A_PAD).astype(BF)
    rope_rot = jnp.concatenate([-rope[..., A_ROPE_DIM // 2:], rope[..., :A_ROPE_DIM // 2]], axis=-1)
    wqr = jnp.concatenate([jnp.zeros_like(nope), rope_rot, zpad], axis=-1).reshape(A_Q_RANK, A_PAD).astype(BF)
    wkv3 = w_kv_up.reshape(A_KV_RANK, A_HEADS, A_NOPE_DIM + A_V_DIM)
    k_nope, v = wkv3[..., :A_NOPE_DIM], wkv3[..., A_NOPE_DIM:]
    zk = jnp.zeros_like(k_nope)
    wk = jnp.concatenate([k_nope, zk], axis=-1).reshape(A_KV_RANK, A_PAD).astype(BF)
    v_even = jnp.concatenate([v, jnp.zeros_like(v)], axis=-1)
    v_odd = jnp.concatenate([jnp.zeros_like(v), v], axis=-1)
    odd = (np.arange(A_HEADS) % 2 == 1)[None, :, None]
    wv = jnp.where(odd, v_odd, v_even).reshape(A_KV_RANK, A_PAD).astype(BF)
    e = np.zeros((LANES, A_PAD), np.float32)
    for hd in range(A_HEADS):
        for j in range(A_ROPE_DIM):
            e[j, hd * LANES + A_NOPE_DIM + j] = 1.0
    return dict(win=win, qg=q_norm[None, :], kvg=kv_norm[None, :], wq=wq, wqr=wqr, wk=wk, wv=wv,
                e=jnp.asarray(e, BF), wo=w_out.astype(BF))


def kernel(x, ev_norm, ev_w_in, ev_q_norm, ev_w_q_up, ev_kv_norm, ev_w_kv_up, ev_w_out, od_norm, od_w_qkv, od_rpb, od_w_out, ffn_norm, ffn_w_gate, ffn_w_up, ffn_w_down, final_norm):
    batch, seq, d = x.shape
    h = x.reshape(batch * seq, d)
    ta, tb = _rope_tables(seq)
    for layer in range(DEPTH):
        i = layer // 2
        if layer % 2 == 0:
            w = _prep_even(ev_w_in[i], ev_q_norm[i], ev_w_q_up[i], ev_kv_norm[i], ev_w_kv_up[i], ev_w_out[i])
            qa, ka, va, qb, kb, vb, qg, kg, vg = _proj_even(h, ev_norm[i][None, :], w, ta, tb, seq)
            o_a = _mla(qa, ka, va, batch, seq)
            o_b = _dilated(qb, kb, vb, qg, kg, vg, batch, seq)
            attn, wo = [o_a, o_b], w["wo"]
        else:
            wqkv = jnp.concatenate([od_w_qkv[i][:, :C_WIDTH] * HEAD_DIM ** -0.5, od_w_qkv[i][:, C_WIDTH:]],
                                   axis=1).astype(BF)
            qkv = _proj_odd(h, od_norm[i][None, :], wqkv)
            attn = [_neighbourhood(qkv, _na_bias_table(od_rpb[i], seq), batch, seq)]
            wo = od_w_out[i].astype(BF)
        h = _post(h, attn, wo, ffn_norm[layer][None, :], ffn_w_gate[layer].astype(BF),
                  ffn_w_up[layer].astype(BF), ffn_w_down[layer].astype(BF),
                  final_g=final_norm[None, :] if layer == DEPTH - 1 else None)
    return h.reshape(batch, seq, d)
```

```python
import functools

import numpy as np
import jax
import jax.numpy as jnp
from jax import lax
from jax.experimental import pallas as pl
from jax.experimental.pallas import tpu as pltpu

BF = jnp.bfloat16
F32 = jnp.float32

D_MODEL = 1024
DEPTH = 4
HEAD_DIM = 64
A_HEADS = 8
A_Q_RANK = 256
A_KV_RANK = 128
A_NOPE_DIM = 64
A_ROPE_DIM = 32
A_V_DIM = 64
B_HEADS = 8
B_BRANCHES = ((128, 1), (512, 4), (2048, 16))
C_HEADS = 16
GRID_W = 64
NA_ROWS = 8
NA_COLS = 16
D_FF = -(-8 * D_MODEL // (3 * 256)) * 256
ROPE_THETA = 10000.0
EPS = 1e-6
NEG_INF = -1e30

LANES = 128
B_WIDTH = B_HEADS * HEAD_DIM
C_WIDTH = C_HEADS * HEAD_DIM
A_PAD = A_HEADS * LANES
IN_COLS = A_Q_RANK + A_KV_RANK + 3 * B_WIDTH + 2 * LANES
COL_QB = A_Q_RANK + A_KV_RANK
COL_KB = COL_QB + B_WIDTH
COL_VB = COL_KB + B_WIDTH
COL_KPE = COL_VB + B_WIDTH
COL_KPE_ROT = COL_KPE + LANES
HALF_WINDOW = 64
NA_QROWS = 2
NA_KROWS = 10
NA_VARIANTS = 5
NA_UNROLL = 16
RES_GROUPS = 16
DIL_QB = 128
DIL_KB = 256
DIL_UNROLL = 8
MLA_GROUP = 4
VMEM_LIMIT = 56 * 1024 * 1024


def _rms(x, g):
    return x * lax.rsqrt(jnp.mean(x * x, axis=-1, keepdims=True) + EPS) * g


def _dot(a, b):
    return jnp.dot(a, b, preferred_element_type=F32)


def _dot_nt(a, b):
    return lax.dot_general(a, b, (((1,), (1,)), ((), ())), preferred_element_type=F32)


def _loop(trips, body):
    if trips == 1:
        body(0, 0)
    else:
        lax.fori_loop(0, trips, body, 0)


def _clip(x, lo, hi):
    return min(max(x, lo), hi) if isinstance(x, int) else jnp.clip(x, lo, hi)


def _select(pred, a, b):
    return (a if pred else b) if isinstance(pred, bool) else jnp.where(pred, a, b)


def _aligned(x, m):
    return x if isinstance(x, int) else pl.multiple_of(x, m)


def _const_spec(shape):
    nd = len(shape)
    return pl.BlockSpec(shape, lambda *_: (0,) * nd, pipeline_mode=pl.Buffered(1))


def _params(n_grid):
    return pltpu.CompilerParams(dimension_semantics=("arbitrary",) * n_grid,
                                vmem_limit_bytes=VMEM_LIMIT)


def _proj_even_kernel(h_ref, g_ref, win_ref, qg_ref, kvg_ref, wq_ref, wqr_ref, wk_ref, wv_ref,
                      e_ref, ta_ref, tb_ref, qa_ref, ka_ref, va_ref, qb_ref, kb_ref, vb_ref,
                      qr_ref, kr_ref, vr_ref, stage_ref):
    xn = _rms(h_ref[...], g_ref[...]).astype(BF)
    z = _dot(xn, win_ref[...])
    cq = _rms(z[:, 0:A_Q_RANK], qg_ref[...]).astype(BF)
    ckv = _rms(z[:, A_Q_RANK:COL_QB], kvg_ref[...]).astype(BF)
    qa = _dot(cq, wq_ref[...])
    qr = _dot(cq, wqr_ref[...])
    cq_t, sq_t = ta_ref[:, 0:LANES], ta_ref[:, LANES:2 * LANES]
    ck_t, sk_t = ta_ref[:, 2 * LANES:3 * LANES], ta_ref[:, 3 * LANES:4 * LANES]
    for hd in range(A_HEADS):
        sl = slice(hd * LANES, (hd + 1) * LANES)
        qa_ref[:, sl] = (qa[:, sl] * cq_t + qr[:, sl] * sq_t).astype(BF)
    kpe = z[:, COL_KPE:COL_KPE + LANES] * ck_t + z[:, COL_KPE_ROT:COL_KPE_ROT + LANES] * sk_t
    ka_ref[...] = (_dot(ckv, wk_ref[...]) + _dot(kpe.astype(BF), e_ref[...])).astype(BF)
    col = lax.broadcasted_iota(jnp.int32, (1, A_PAD), 1)
    own_half = ((col // HEAD_DIM) % 2) == ((col // LANES) % 2)
    va_ref[...] = (_dot(ckv, wv_ref[...]) + jnp.where(own_half, 0.0, 1.0)).astype(BF)
    c_t, s_hi, s_lo = tb_ref[:, 0:LANES], tb_ref[:, LANES:2 * LANES], tb_ref[:, 2 * LANES:3 * LANES]
    for p in range(B_WIDTH // LANES):
        for idx, (col, out) in enumerate(((COL_QB, qb_ref), (COL_KB, kb_ref))):
            xs = z[:, col + p * LANES: col + (p + 1) * LANES]
            r = (xs * c_t + pltpu.roll(xs, HEAD_DIM // 2, 1) * s_hi
                 + pltpu.roll(xs, LANES - HEAD_DIM // 2, 1) * s_lo)
            out[:, p * LANES:(p + 1) * LANES] = r.astype(BF)
            stage_ref[idx, p] = r
        stage_ref[2, p] = z[:, COL_VB + p * LANES:COL_VB + (p + 1) * LANES]
    vb_ref[...] = z[:, COL_VB:COL_KPE].astype(BF)
    rows = stage_ref.shape[2] // RES_GROUPS
    for idx, out in enumerate((qr_ref, kr_ref, vr_ref)):
        for p in range(B_WIDTH // LANES):
            for r in range(RES_GROUPS):
                out[r, :, p * LANES:(p + 1) * LANES] = (
                    stage_ref[idx, p, pl.ds(r, rows, stride=RES_GROUPS), :].astype(BF))


def _proj_even(h, g, w, ta, tb, seq, tm=512):
    T = h.shape[0]
    nseq = seq // tm
    row = lambda c: pl.BlockSpec((tm, c), lambda i: (i, 0))
    tab = lambda c: pl.BlockSpec((tm, c), lambda i: (i % nseq, 0))
    consts = [g, w["win"], w["qg"], w["kvg"], w["wq"], w["wqr"], w["wk"], w["wv"], w["e"]]
    out_cols = (A_PAD, A_PAD, A_PAD, B_WIDTH, B_WIDTH, B_WIDTH)
    grouped = pl.BlockSpec((None, RES_GROUPS, tm // RES_GROUPS, B_WIDTH), lambda i: (i // nseq, 0, i % nseq, 0))
    grouped_shape = jax.ShapeDtypeStruct((T // seq, RES_GROUPS, seq // RES_GROUPS, B_WIDTH), BF)
    return pl.pallas_call(
        _proj_even_kernel,
        grid=(T // tm,),
        in_specs=[row(D_MODEL)] + [_const_spec(c.shape) for c in consts] + [tab(4 * LANES), tab(3 * LANES)],
        out_specs=[row(c) for c in out_cols] + [grouped] * 3,
        out_shape=[jax.ShapeDtypeStruct((T, c), BF) for c in out_cols] + [grouped_shape] * 3,
        scratch_shapes=[pltpu.VMEM((3, B_WIDTH // LANES, tm, LANES), F32)],
        compiler_params=_params(1),
        name="proj_even",
    )(h, *consts, ta, tb)


def _mla_kernel(q_ref, k_ref, v_ref, o_ref):
    low = _low_half(q_ref.shape[0])
    tile = lambda hd: slice(hd * LANES, (hd + 1) * LANES)
    for g in range(A_HEADS // MLA_GROUP):
        heads = range(g * MLA_GROUP, (g + 1) * MLA_GROUP)
        scores = [_dot_nt(q_ref[:, tile(hd)], k_ref[:, tile(hd)]) for hd in heads]
        probs = [_exp_rows(s)[0] for s in scores]
        outs = [_dot(p, v_ref[:, tile(hd)]) for p, hd in zip(probs, heads)]
        for pair in range(MLA_GROUP // 2):
            o_ref[:, tile(g * MLA_GROUP // 2 + pair)] = _pair_finish(outs[2 * pair], outs[2 * pair + 1], low).astype(BF)


def _mla(qa, ka, va, batch, seq, tq=256):
    q3, k3, v3 = (t.reshape(batch, seq, A_PAD) for t in (qa, ka, va))
    out = pl.pallas_call(
        _mla_kernel,
        grid=(batch, seq // tq),
        in_specs=[pl.BlockSpec((None, tq, A_PAD), lambda b, i: (b, i, 0)),
                  pl.BlockSpec((None, seq, A_PAD), lambda b, i: (b, 0, 0)),
                  pl.BlockSpec((None, seq, A_PAD), lambda b, i: (b, 0, 0))],
        out_specs=pl.BlockSpec((None, tq, B_WIDTH), lambda b, i: (b, i, 0)),
        out_shape=jax.ShapeDtypeStruct((batch, seq, A_HEADS * A_V_DIM), BF),
        compiler_params=_params(2),
        name="mla_attention",
    )(q3, k3, v3)
    return out.reshape(batch * seq, A_HEADS * A_V_DIM)


def _low_half(n):
    return lax.broadcasted_iota(jnp.int32, (n, LANES), 1) < HEAD_DIM


def _pair_scores(q2, k2, bias, low_q):
    zero = jnp.zeros_like(q2)
    return [_dot_nt(jnp.where(low_q, q2, zero), k2) + bias, _dot_nt(jnp.where(low_q, zero, q2), k2) + bias]


def _pair_values(v2, low_k):
    one = jnp.ones_like(v2)
    return [jnp.where(low_k, v2, one), jnp.where(low_k, one, v2)]


def _exp_rows(s):
    m = jnp.max(s, axis=-1, keepdims=True)
    return jnp.exp(s - m).astype(BF), m


def _pair_finish(o0, o1, low_q, m0=None, m1=None):
    den = pltpu.roll(jnp.where(low_q, o1, o0), HEAD_DIM, 1)
    out = jnp.where(low_q, o0, o1) / den
    if m0 is None:
        return out
    return out, jnp.where(low_q, m0, m1) + jnp.log(den)


def _attend_blocks(blocks, low_q, low_k):
    scores = [_pair_scores(q2, k2, bias, low_q) for q2, k2, _, bias in blocks]
    probs = [[_exp_rows(s) for s in pair] for pair in scores]
    results = []
    for (_, _, v2, _), pair in zip(blocks, probs):
        vals = _pair_values(v2, low_k)
        o0, o1 = _dot(pair[0][0], vals[0]), _dot(pair[1][0], vals[1])
        results.append(_pair_finish(o0, o1, low_q, pair[0][1], pair[1][1]))
    return results


def _grouped_branch(qg_ref, kg_ref, vg_ref, mask_ref, mask_base, o_ref, l_ref, dil):
    n16 = qg_ref.shape[1]
    per, jq, jk, n_jb = _grouped_geometry(dil, n16)
    nq, nk = per * jq, per * jk
    low_q, low_k = _low_half(nq), _low_half(nk)
    r_unroll = max(1, DIL_UNROLL // n_jb)

    def body(i, carry):
        blocks, where = [], []
        for ru in range(r_unroll):
            rr = i * r_unroll + ru
            for jb in range(n_jb):
                q0, k0 = _grouped_window(jb, per, jq, jk, n16)
                cat = lambda ref, lo, n: jnp.concatenate(
                    [ref[rr + dil * u, lo:lo + n, :] for u in range(per)], axis=0)
                blocks.append((cat(qg_ref, q0, jq), cat(kg_ref, k0, jk), cat(vg_ref, k0, jk),
                               mask_ref[mask_base + jb, :, 0:nk]))
                where.append((rr, q0))
        for (rr, q0), (o, l) in zip(where, _attend_blocks(blocks, low_q, low_k)):
            for u in range(per):
                o_ref[rr + dil * u, q0:q0 + jq, :] = o[u * jq:(u + 1) * jq]
                l_ref[rr + dil * u, q0:q0 + jq, :] = l[u * jq:(u + 1) * jq]
        return carry

    _loop(dil // r_unroll, body)


def _grouped_geometry(dil, n16):
    per = RES_GROUPS // dil
    jq = DIL_QB // per
    jk = min(DIL_KB, n16 * per) // per
    return per, jq, jk, n16 // jq


def _grouped_window(jb, per, jq, jk, n16):
    q0 = jq * jb
    return q0, int(np.clip(q0 - HALF_WINDOW // per, 0, n16 - jk))


def _dilated_masks(seq):
    n16 = seq // RES_GROUPS
    neg = np.float32(NEG_INF)
    q = np.arange(DIL_QB)[:, None]
    k = np.arange(DIL_KB)[None, :]
    tiles = []
    n_blocks = seq // DIL_QB
    for a in (0, 1, n_blocks - 1):
        qs = a * DIL_QB
        ks = int(np.clip(qs - HALF_WINDOW, 0, seq - DIL_KB))
        tiles.append(np.where(np.abs((k + ks) - (q + qs)) <= HALF_WINDOW, 0, neg))
    for _, dil in B_BRANCHES[1:]:
        per, jq, jk, n_jb = _grouped_geometry(dil, n16)
        pos = lambda i, n: per * (i % n) + i // n
        for jb in range(n_jb):
            q0, k0 = _grouped_window(jb, per, jq, jk, n16)
            rel = (pos(k, jk) + per * k0) - (pos(q, jq) + per * q0)
            tile = np.where(np.abs(rel) <= HALF_WINDOW, 0, neg)
            tile[:, per * jk:] = neg
            tiles.append(tile)
    return jnp.asarray(np.stack(tiles).astype(np.float32))


def _dilated_kernel(qn_ref, kn_ref, vn_ref, qg_ref, kg_ref, vg_ref, mask_ref, out_ref,
                    o1_ref, l1_ref, o2_ref, l2_ref, o3_ref, l3_ref, mix_ref):
    seq = qn_ref.shape[0]
    n16 = seq // RES_GROUPS
    n_blocks = seq // DIL_QB
    low_q, low_k = _low_half(DIL_QB), _low_half(DIL_KB)

    def body(i, carry):
        blocks, starts = [], []
        for u in range(DIL_UNROLL):
            a = i * DIL_UNROLL + u
            qs = _aligned(a * DIL_QB, DIL_QB)
            ks = _aligned(_clip(a * DIL_QB - HALF_WINDOW, 0, seq - DIL_KB), HALF_WINDOW)
            variant = _select(a == 0, 0, _select(a == n_blocks - 1, 2, 1))
            blocks.append((qn_ref[pl.ds(qs, DIL_QB), :], kn_ref[pl.ds(ks, DIL_KB), :],
                           vn_ref[pl.ds(ks, DIL_KB), :], mask_ref[variant]))
            starts.append(qs)
        for qs, (o, l) in zip(starts, _attend_blocks(blocks, low_q, low_k)):
            o1_ref[pl.ds(qs, DIL_QB), :] = o
            l1_ref[pl.ds(qs, DIL_QB), :] = l
        return carry

    _loop(n_blocks // DIL_UNROLL, body)
    base4 = 3
    base16 = base4 + _grouped_geometry(B_BRANCHES[1][1], n16)[3]
    _grouped_branch(qg_ref, kg_ref, vg_ref, mask_ref, base4, o2_ref, l2_ref, B_BRANCHES[1][1])
    _grouped_branch(qg_ref, kg_ref, vg_ref, mask_ref, base16, o3_ref, l3_ref, B_BRANCHES[2][1])
    for r in range(RES_GROUPS):
        rows = pl.ds(r, n16, stride=RES_GROUPS)
        la, lb, lc = l1_ref[rows, :], l2_ref[r], l3_ref[r]
        m = jnp.maximum(jnp.maximum(la, lb), lc)
        ea, eb, ec = jnp.exp(la - m), jnp.exp(lb - m), jnp.exp(lc - m)
        mix_ref[rows, :] = (ea * o1_ref[rows, :] + eb * o2_ref[r] + ec * o3_ref[r]) / (ea + eb + ec)
    out_ref[...] = mix_ref[...].astype(BF)


def _dilated(qn, kn, vn, qg, kg, vg, batch, seq):
    n16 = seq // RES_GROUPS
    nat = pl.BlockSpec((None, seq, LANES), lambda b, hp: (b, 0, hp))
    grp = pl.BlockSpec((None, RES_GROUPS, n16, LANES), lambda b, hp: (b, 0, 0, hp))
    nat3 = lambda t: t.reshape(batch, seq, B_WIDTH)
    masks = _dilated_masks(seq)
    out = pl.pallas_call(
        _dilated_kernel,
        grid=(batch, B_WIDTH // LANES),
        in_specs=[nat, nat, nat, grp, grp, grp, _const_spec(masks.shape)],
        out_specs=nat,
        out_shape=jax.ShapeDtypeStruct((batch, seq, B_WIDTH), BF),
        scratch_shapes=[pltpu.VMEM((seq, LANES), F32), pltpu.VMEM((seq, LANES), F32)]
        + [pltpu.VMEM((RES_GROUPS, n16, LANES), F32)] * 4 + [pltpu.VMEM((seq, LANES), F32)],
        compiler_params=_params(2),
        name="dilated_attention",
    )(nat3(qn), nat3(kn), nat3(vn), qg, kg, vg, masks)
    return out.reshape(batch * seq, B_WIDTH)


def _proj_odd_kernel(h_ref, g_ref, w_ref, o_ref):
    xn = _rms(h_ref[...], g_ref[...]).astype(BF)
    o_ref[...] = _dot(xn, w_ref[...]).astype(BF)


def _proj_odd(h, g, w, tm=512):
    T = h.shape[0]
    n = w.shape[1]
    return pl.pallas_call(
        _proj_odd_kernel,
        grid=(T // tm,),
        in_specs=[pl.BlockSpec((tm, D_MODEL), lambda i: (i, 0)), _const_spec(g.shape), _const_spec(w.shape)],
        out_specs=pl.BlockSpec((tm, n), lambda i: (i, 0)),
        out_shape=jax.ShapeDtypeStruct((T, n), BF),
        compiler_params=_params(1),
        name="proj_odd",
    )(h, g, w)


def _na_kernel(q_ref, k_ref, v_ref, b_ref, o_ref, vm_ref, *, seq):
    rows = seq // GRID_W
    nq = NA_QROWS * GRID_W
    nk = NA_KROWS * GRID_W
    low = _low_half(nq)
    vals = _pair_values(v_ref[...], _low_half(seq))
    vm_ref[0] = vals[0]
    vm_ref[1] = vals[1]
    n_blocks = rows // NA_QROWS

    def body(i, carry):
        scores, where = [], []
        for u in range(NA_UNROLL):
            a = i * NA_UNROLL + u
            qs = _aligned(a * nq, nq)
            ks = _aligned(_clip(NA_QROWS * a - NA_ROWS // 2, 0, rows - NA_KROWS) * GRID_W, GRID_W)
            var = _select(a < 2, a, _select(a >= n_blocks - 2, a - (n_blocks - NA_VARIANTS), 2))
            q2 = q_ref[pl.ds(qs, nq), :]
            k2 = k_ref[pl.ds(ks, nk), :]
            zero = jnp.zeros_like(q2)
            scores.append([_dot_nt(jnp.where(low, q2, zero), k2) + b_ref[0, var],
                           _dot_nt(jnp.where(low, zero, q2), k2) + b_ref[1, var]])
            where.append((qs, ks))
        probs = [[_exp_rows(s)[0] for s in pair] for pair in scores]
        for (qs, ks), pair in zip(where, probs):
            o0 = _dot(pair[0], vm_ref[0, pl.ds(ks, nk), :])
            o1 = _dot(pair[1], vm_ref[1, pl.ds(ks, nk), :])
            o_ref[pl.ds(qs, nq), :] = _pair_finish(o0, o1, low).astype(BF)
        return carry

    _loop(n_blocks // NA_UNROLL, body)


def _na_bias_table(rpb, seq):
    rows = seq // GRID_W
    n_blocks = rows // NA_QROWS
    col = np.arange(GRID_W)
    cs = np.clip(col - NA_COLS // 2, 0, GRID_W - NA_COLS)
    col_valid = (col[None, :] >= cs[:, None]) & (col[None, :] < cs[:, None] + NA_COLS)
    col_off = np.clip(col[None, :] - col[:, None] + NA_COLS - 1, 0, 2 * NA_COLS - 2)
    t2 = jnp.where(col_valid[None, None], rpb[:, :, col_off], NEG_INF)
    t2 = jnp.concatenate([t2, jnp.full_like(t2[:, :1], NEG_INF)], axis=1)
    n_off = 2 * NA_ROWS - 1
    idx = np.full((NA_VARIANTS, NA_QROWS, NA_KROWS), n_off, np.int32)
    for v, a in enumerate((0, 1, 2, n_blocks - 2, n_blocks - 1)):
        ws = int(np.clip(NA_QROWS * a - NA_ROWS // 2, 0, rows - NA_KROWS))
        for qr in range(NA_QROWS):
            r = NA_QROWS * a + qr
            rs = int(np.clip(r - NA_ROWS // 2, 0, rows - NA_ROWS))
            for kr in range(NA_KROWS):
                krow = ws + kr
                if rs <= krow < rs + NA_ROWS:
                    idx[v, qr, kr] = krow - r + NA_ROWS - 1
    big = t2[:, idx]
    big = big.transpose(0, 1, 2, 4, 3, 5)
    return big.reshape(C_HEADS, NA_VARIANTS, NA_QROWS * GRID_W, NA_KROWS * GRID_W)


def _neighbourhood(qkv, bias, batch, seq):
    qkv3 = qkv.reshape(batch, seq, 3 * C_WIDTH)
    n_hp = C_WIDTH // LANES
    blk = lambda off: pl.BlockSpec((None, seq, LANES), lambda hp, b: (b, 0, off + hp))
    out = pl.pallas_call(
        functools.partial(_na_kernel, seq=seq),
        grid=(n_hp, batch),
        in_specs=[blk(0), blk(n_hp), blk(2 * n_hp),
                  pl.BlockSpec((2,) + bias.shape[1:], lambda hp, b: (hp, 0, 0, 0))],
        out_specs=pl.BlockSpec((None, seq, LANES), lambda hp, b: (b, 0, hp)),
        out_shape=jax.ShapeDtypeStruct((batch, seq, C_WIDTH), BF),
        scratch_shapes=[pltpu.VMEM((2, seq, LANES), BF)],
        compiler_params=_params(2),
        name="neighbourhood_attention",
    )(qkv3, qkv3, qkv3, bias)
    return out.reshape(batch * seq, C_WIDTH)


def _post_kernel(*refs, n_attn, final):
    h_ref = refs[0]
    attn = refs[1:1 + n_attn]
    wo_ref, g_ref, wg_ref, wu_ref, wd_ref = refs[1 + n_attn:6 + n_attn]
    fg_ref = refs[6 + n_attn] if final else None
    out_ref = refs[-1]
    h1 = h_ref[...]
    off = 0
    for a in attn:
        width = a.shape[1]
        h1 = h1 + _dot(a[...], wo_ref[off:off + width, :])
        off += width
    xn = _rms(h1, g_ref[...]).astype(BF)
    gate = _dot(xn, wg_ref[...])
    up = _dot(xn, wu_ref[...])
    act = (gate * (1.0 / (1.0 + jnp.exp(-gate))) * up).astype(BF)
    y = h1 + _dot(act, wd_ref[...])
    if final:
        y = _rms(y, fg_ref[...])
    out_ref[...] = y


def _post(h, attn, wo, g, wg, wu, wd, final_g=None, tm=512):
    T = h.shape[0]
    final = final_g is not None
    row = lambda c: pl.BlockSpec((tm, c), lambda i: (i, 0))
    consts = [wo, g, wg, wu, wd] + ([final_g] if final else [])
    return pl.pallas_call(
        functools.partial(_post_kernel, n_attn=len(attn), final=final),
        grid=(T // tm,),
        in_specs=[row(D_MODEL)] + [row(a.shape[1]) for a in attn] + [_const_spec(c.shape) for c in consts],
        out_specs=row(D_MODEL),
        out_shape=jax.ShapeDtypeStruct((T, D_MODEL), F32),
        compiler_params=_params(1),
        name="outproj_ffn",
    )(h, *attn, *consts)


def _rope_tables(seq):
    pos = jnp.arange(seq, dtype=F32)

    def cs(dim):
        inv = ROPE_THETA ** (-jnp.arange(0, dim, 2, dtype=F32) / dim)
        ang = pos[:, None] * inv[None, :]
        return jnp.cos(ang), jnp.sin(ang)

    cos_a, sin_a = cs(A_ROPE_DIM)
    cos2, sin2 = jnp.tile(cos_a, (1, 2)), jnp.tile(sin_a, (1, 2))
    zeros = lambda n: jnp.zeros((seq, n), F32)
    scale = (A_NOPE_DIM + A_ROPE_DIM) ** -0.5
    pad = LANES - A_NOPE_DIM - A_ROPE_DIM
    cq = jnp.concatenate([jnp.full((seq, A_NOPE_DIM), scale, F32), cos2 * scale, zeros(pad)], axis=1)
    sq = jnp.concatenate([zeros(A_NOPE_DIM), sin2 * scale, zeros(pad)], axis=1)
    ck = jnp.concatenate([cos2, zeros(LANES - A_ROPE_DIM)], axis=1)
    sk = jnp.concatenate([sin2, zeros(LANES - A_ROPE_DIM)], axis=1)
    ta = jnp.concatenate([cq, sq, ck, sk], axis=1)
    cos_b, sin_b = cs(HEAD_DIM)
    zb = jnp.zeros_like(sin_b)
    c = jnp.tile(cos_b, (1, 4))
    s_hi = jnp.tile(jnp.concatenate([zb, sin_b], axis=1), (1, 2))
    s_lo = jnp.tile(jnp.concatenate([-sin_b, zb], axis=1), (1, 2))
    tb = jnp.concatenate([c, s_hi, s_lo], axis=1)
    return ta, tb


def _rot_half_cols(w):
    half = w.shape[1] // 2
    return jnp.concatenate([-w[:, half:], w[:, :half]], axis=1)


def _prep_even(w_in, q_norm, w_q_up, kv_norm, w_kv_up, w_out):
    d = w_in.shape[0]
    c_q, c_kv, k_pe, qkv_b = jnp.split(w_in, [A_Q_RANK, COL_QB, COL_QB + A_ROPE_DIM], axis=1)
    q_b, k_b, v_b = jnp.split(qkv_b, 3, axis=1)
    padc = jnp.zeros((d, LANES - A_ROPE_DIM), F32)
    win = jnp.concatenate([c_q, c_kv, q_b * HEAD_DIM ** -0.5, k_b, v_b,
                           k_pe, padc, _rot_half_cols(k_pe), padc], axis=1).astype(BF)
    wq3 = w_q_up.reshape(A_Q_RANK, A_HEADS, A_NOPE_DIM + A_ROPE_DIM)
    nope, rope = wq3[..., :A_NOPE_DIM], wq3[..., A_NOPE_DIM:]
    zpad = jnp.zeros((A_Q_RANK, A_HEADS, LANES - A_NOPE_DIM - A_ROPE_DIM), F32)
    wq = jnp.concatenate([nope, rope, zpad], axis=-1).reshape(A_Q_RANK, A_PAD).astype(BF)
    rope_rot = jnp.concatenate([-rope[..., A_ROPE_DIM // 2:], rope[..., :A_ROPE_DIM // 2]], axis=-1)
    wqr = jnp.concatenate([jnp.zeros_like(nope), rope_rot, zpad], axis=-1).reshape(A_Q_RANK, A_PAD).astype(BF)
    wkv3 = w_kv_up.reshape(A_KV_RANK, A_HEADS, A_NOPE_DIM + A_V_DIM)
    k_nope, v = wkv3[..., :A_NOPE_DIM], wkv3[..., A_NOPE_DIM:]
    zk = jnp.zeros_like(k_nope)
    wk = jnp.concatenate([k_nope, zk], axis=-1).reshape(A_KV_RANK, A_PAD).astype(BF)
    v_even = jnp.concatenate([v, jnp.zeros_like(v)], axis=-1)
    v_odd = jnp.concatenate([jnp.zeros_like(v), v], axis=-1)
    odd = (np.arange(A_HEADS) % 2 == 1)[None, :, None]
    wv = jnp.where(odd, v_odd, v_even).reshape(A_KV_RANK, A_PAD).astype(BF)
    e = np.zeros((LANES, A_PAD), np.float32)
    for hd in range(A_HEADS):
        for j in range(A_ROPE_DIM):
            e[j, hd * LANES + A_NOPE_DIM + j] = 1.0
    return dict(win=win, qg=q_norm[None, :], kvg=kv_norm[None, :], wq=wq, wqr=wqr, wk=wk, wv=wv,
                e=jnp.asarray(e, BF), wo=w_out.astype(BF))


def kernel(x, ev_norm, ev_w_in, ev_q_norm, ev_w_q_up, ev_kv_norm, ev_w_kv_up, ev_w_out, od_norm, od_w_qkv, od_rpb, od_w_out, ffn_norm, ffn_w_gate, ffn_w_up, ffn_w_down, final_norm):
    batch, seq, d = x.shape
    h = x.reshape(batch * seq, d)
    ta, tb = _rope_tables(seq)
    for layer in range(DEPTH):
        i = layer // 2
        if layer % 2 == 0:
            w = _prep_even(ev_w_in[i], ev_q_norm[i], ev_w_q_up[i], ev_kv_norm[i], ev_w_kv_up[i], ev_w_out[i])
            qa, ka, va, qb, kb, vb, qg, kg, vg = _proj_even(h, ev_norm[i][None, :], w, ta, tb, seq)
            o_a = _mla(qa, ka, va, batch, seq)
            o_b = _dilated(qb, kb, vb, qg, kg, vg, batch, seq)
            attn, wo = [o_a, o_b], w["wo"]
        else:
            wqkv = jnp.concatenate([od_w_qkv[i][:, :C_WIDTH] * HEAD_DIM ** -0.5, od_w_qkv[i][:, C_WIDTH:]],
                                   axis=1).astype(BF)
            qkv = _proj_odd(h, od_norm[i][None, :], wqkv)
            attn = [_neighbourhood(qkv, _na_bias_table(od_rpb[i], seq), batch, seq)]
            wo = od_w_out[i].astype(BF)
        h = _post(h, attn, wo, ffn_norm[layer][None, :], ffn_w_gate[layer].astype(BF),
                  ffn_w_up[layer].astype(BF), ffn_w_down[layer].astype(BF),
                  final_g=final_norm[None, :] if layer == DEPTH - 1 else None)
    return h.reshape(batch, seq, d)
```

```python
import functools

import numpy as np
import jax
import jax.numpy as jnp
from jax import lax
from jax.experimental import pallas as pl
from jax.experimental.pallas import tpu as pltpu

BF = jnp.bfloat16
F32 = jnp.float32

D_MODEL = 1024
DEPTH = 4
HEAD_DIM = 64
A_HEADS = 8
A_Q_RANK = 256
A_KV_RANK = 128
A_NOPE_DIM = 64
A_ROPE_DIM = 32
A_V_DIM = 64
B_HEADS = 8
B_BRANCHES = ((128, 1), (512, 4), (2048, 16))
C_HEADS = 16
GRID_W = 64
NA_ROWS = 8
NA_COLS = 16
D_FF = -(-8 * D_MODEL // (3 * 256)) * 256
ROPE_THETA = 10000.0
EPS = 1e-6
NEG_INF = -1e30
LOG2E = float(np.log2(np.e))
LN2 = float(np.log(2.0))

LANES = 128
B_WIDTH = B_HEADS * HEAD_DIM
C_WIDTH = C_HEADS * HEAD_DIM
A_PAD = A_HEADS * LANES
IN_COLS = A_Q_RANK + A_KV_RANK + 3 * B_WIDTH + 2 * LANES
COL_QB = A_Q_RANK + A_KV_RANK
COL_KB = COL_QB + B_WIDTH
COL_VB = COL_KB + B_WIDTH
COL_KPE = COL_VB + B_WIDTH
COL_KPE_ROT = COL_KPE + LANES
HALF_WINDOW = 64
NA_QROWS = 2
NA_KROWS = 10
NA_VARIANTS = 5
NA_UNROLL = 16
RES_GROUPS = 16
DIL_QB = 128
DIL_KB = 256
DIL_UNROLL = 8
MLA_GROUP = 2
VMEM_LIMIT = 56 * 1024 * 1024


def _rms(x, g):
    return x * lax.rsqrt(jnp.mean(x * x, axis=-1, keepdims=True) + EPS) * g


def _dot(a, b):
    return jnp.dot(a, b, preferred_element_type=F32)


def _dot_nt(a, b):
    return lax.dot_general(a, b, (((1,), (1,)), ((), ())), preferred_element_type=F32)


def _loop(trips, body):
    if trips == 1:
        body(0, 0)
    else:
        lax.fori_loop(0, trips, body, 0)


def _clip(x, lo, hi):
    return min(max(x, lo), hi) if isinstance(x, int) else jnp.clip(x, lo, hi)


def _select(pred, a, b):
    return (a if pred else b) if isinstance(pred, bool) else jnp.where(pred, a, b)


def _aligned(x, m):
    return x if isinstance(x, int) else pl.multiple_of(x, m)


def _const_spec(shape):
    nd = len(shape)
    return pl.BlockSpec(shape, lambda *_: (0,) * nd, pipeline_mode=pl.Buffered(1))


def _params(n_grid):
    return pltpu.CompilerParams(dimension_semantics=("arbitrary",) * n_grid,
                                vmem_limit_bytes=VMEM_LIMIT)


def _proj_even_kernel(h_ref, g_ref, win_ref, qg_ref, kvg_ref, wq_ref, wqr_ref, wk_ref, wv_ref,
                      e_ref, ta_ref, tb_ref, qa_ref, ka_ref, va_ref, qb_ref, kb_ref, vb_ref,
                      qr_ref, kr_ref, vr_ref, stage_ref):
    xn = _rms(h_ref[...], g_ref[...]).astype(BF)
    z = _dot(xn, win_ref[...])
    cq = _rms(z[:, 0:A_Q_RANK], qg_ref[...]).astype(BF)
    ckv = _rms(z[:, A_Q_RANK:COL_QB], kvg_ref[...]).astype(BF)
    qa = _dot(cq, wq_ref[...])
    qr = _dot(cq, wqr_ref[...])
    cq_t, sq_t = ta_ref[:, 0:LANES], ta_ref[:, LANES:2 * LANES]
    ck_t, sk_t = ta_ref[:, 2 * LANES:3 * LANES], ta_ref[:, 3 * LANES:4 * LANES]
    for hd in range(A_HEADS):
        sl = slice(hd * LANES, (hd + 1) * LANES)
        qa_ref[:, sl] = (qa[:, sl] * cq_t + qr[:, sl] * sq_t).astype(BF)
    kpe = z[:, COL_KPE:COL_KPE + LANES] * ck_t + z[:, COL_KPE_ROT:COL_KPE_ROT + LANES] * sk_t
    ka_ref[...] = (_dot(ckv, wk_ref[...]) + _dot(kpe.astype(BF), e_ref[...])).astype(BF)
    col = lax.broadcasted_iota(jnp.int32, (1, A_PAD), 1)
    own_half = ((col // HEAD_DIM) % 2) == ((col // LANES) % 2)
    va_ref[...] = (_dot(ckv, wv_ref[...]) + jnp.where(own_half, 0.0, 1.0)).astype(BF)
    for p in range(B_WIDTH // LANES):
        for idx, (col, out) in enumerate(((COL_QB, qb_ref), (COL_KB, kb_ref))):
            t0 = (1 - idx) * 3 * LANES
            c_t, s_hi, s_lo = (tb_ref[:, t0 + n * LANES:t0 + (n + 1) * LANES] for n in range(3))
            xs = z[:, col + p * LANES: col + (p + 1) * LANES]
            r = (xs * c_t + pltpu.roll(xs, HEAD_DIM // 2, 1) * s_hi
                 + pltpu.roll(xs, LANES - HEAD_DIM // 2, 1) * s_lo)
            out[:, p * LANES:(p + 1) * LANES] = r.astype(BF)
            stage_ref[idx, p] = r
        stage_ref[2, p] = z[:, COL_VB + p * LANES:COL_VB + (p + 1) * LANES]
    vb_ref[...] = z[:, COL_VB:COL_KPE].astype(BF)
    rows = stage_ref.shape[2] // RES_GROUPS
    for idx, out in enumerate((qr_ref, kr_ref, vr_ref)):
        for p in range(B_WIDTH // LANES):
            for r in range(RES_GROUPS):
                out[r, :, p * LANES:(p + 1) * LANES] = (
                    stage_ref[idx, p, pl.ds(r, rows, stride=RES_GROUPS), :].astype(BF))


def _proj_even(h, g, w, ta, tb, seq, tm=512):
    T = h.shape[0]
    nseq = seq // tm
    row = lambda c: pl.BlockSpec((tm, c), lambda i: (i, 0))
    tab = lambda c: pl.BlockSpec((tm, c), lambda i: (i % nseq, 0))
    consts = [g, w["win"], w["qg"], w["kvg"], w["wq"], w["wqr"], w["wk"], w["wv"], w["e"]]
    out_cols = (A_PAD, A_PAD, A_PAD, B_WIDTH, B_WIDTH, B_WIDTH)
    grouped = pl.BlockSpec((None, RES_GROUPS, tm // RES_GROUPS, B_WIDTH), lambda i: (i // nseq, 0, i % nseq, 0))
    grouped_shape = jax.ShapeDtypeStruct((T // seq, RES_GROUPS, seq // RES_GROUPS, B_WIDTH), BF)
    return pl.pallas_call(
        _proj_even_kernel,
        grid=(T // tm,),
        in_specs=[row(D_MODEL)] + [_const_spec(c.shape) for c in consts] + [tab(4 * LANES), tab(6 * LANES)],
        out_specs=[row(c) for c in out_cols] + [grouped] * 3,
        out_shape=[jax.ShapeDtypeStruct((T, c), BF) for c in out_cols] + [grouped_shape] * 3,
        scratch_shapes=[pltpu.VMEM((3, B_WIDTH // LANES, tm, LANES), F32)],
        compiler_params=_params(1),
        name="proj_even",
    )(h, *consts, ta, tb)


def _mla_kernel(q_ref, k_ref, v_ref, o_ref):
    low = _low_half(q_ref.shape[0])
    tile = lambda hd: slice(hd * LANES, (hd + 1) * LANES)
    for g in range(A_HEADS // MLA_GROUP):
        heads = range(g * MLA_GROUP, (g + 1) * MLA_GROUP)
        scores = [_dot_nt(q_ref[:, tile(hd)], k_ref[:, tile(hd)]) for hd in heads]
        probs = [_exp_rows(s)[0] for s in scores]
        outs = [_dot(p, v_ref[:, tile(hd)]) for p, hd in zip(probs, heads)]
        for pair in range(MLA_GROUP // 2):
            o_ref[:, tile(g * MLA_GROUP // 2 + pair)] = _pair_finish(outs[2 * pair], outs[2 * pair + 1], low).astype(BF)


def _mla(qa, ka, va, batch, seq, tq=512):
    q3, k3, v3 = (t.reshape(batch, seq, A_PAD) for t in (qa, ka, va))
    out = pl.pallas_call(
        _mla_kernel,
        grid=(batch, seq // tq),
        in_specs=[pl.BlockSpec((None, tq, A_PAD), lambda b, i: (b, i, 0)),
                  pl.BlockSpec((None, seq, A_PAD), lambda b, i: (b, 0, 0)),
                  pl.BlockSpec((None, seq, A_PAD), lambda b, i: (b, 0, 0))],
        out_specs=pl.BlockSpec((None, tq, B_WIDTH), lambda b, i: (b, i, 0)),
        out_shape=jax.ShapeDtypeStruct((batch, seq, A_HEADS * A_V_DIM), BF),
        compiler_params=_params(2),
        name="mla_attention",
    )(q3, k3, v3)
    return out.reshape(batch * seq, A_HEADS * A_V_DIM)


def _low_half(n):
    return lax.broadcasted_iota(jnp.int32, (n, LANES), 1) < HEAD_DIM


def _pair_scores(q2, k2, bias, low_q):
    zero = jnp.zeros_like(q2)
    return [_dot_nt(jnp.where(low_q, q2, zero), k2) + bias, _dot_nt(jnp.where(low_q, zero, q2), k2) + bias]


def _pair_values(v2, low_k):
    one = jnp.ones_like(v2)
    return [jnp.where(low_k, v2, one), jnp.where(low_k, one, v2)]


def _exp_rows(s):
    m = jnp.max(s, axis=-1, keepdims=True)
    return jnp.exp2(s - m).astype(BF), m


def _pair_finish(o0, o1, low_q, m0=None, m1=None):
    den = pltpu.roll(jnp.where(low_q, o1, o0), HEAD_DIM, 1)
    out = jnp.where(low_q, o0, o1) / den
    if m0 is None:
        return out
    return out, jnp.where(low_q, m0, m1) * LN2 + jnp.log(den)


def _attend_blocks(blocks, low_q, low_k):
    scores = [_pair_scores(q2, k2, bias, low_q) for q2, k2, _, bias in blocks]
    probs = [[_exp_rows(s) for s in pair] for pair in scores]
    results = []
    for (_, _, v2, _), pair in zip(blocks, probs):
        vals = _pair_values(v2, low_k)
        o0, o1 = _dot(pair[0][0], vals[0]), _dot(pair[1][0], vals[1])
        results.append(_pair_finish(o0, o1, low_q, pair[0][1], pair[1][1]))
    return results


def _grouped_branch(qg_ref, kg_ref, vg_ref, mask_ref, mask_base, o_ref, l_ref, dil):
    n16 = qg_ref.shape[1]
    per, jq, jk, n_jb = _grouped_geometry(dil, n16)
    nq, nk = per * jq, per * jk
    low_q, low_k = _low_half(nq), _low_half(nk)
    r_unroll = max(1, DIL_UNROLL // n_jb)

    def body(i, carry):
        blocks, where = [], []
        for ru in range(r_unroll):
            rr = i * r_unroll + ru
            for jb in range(n_jb):
                q0, k0 = _grouped_window(jb, per, jq, jk, n16)
                cat = lambda ref, lo, n: jnp.concatenate(
                    [ref[rr + dil * u, lo:lo + n, :] for u in range(per)], axis=0)
                blocks.append((cat(qg_ref, q0, jq), cat(kg_ref, k0, jk), cat(vg_ref, k0, jk),
                               mask_ref[mask_base + jb, :, 0:nk]))
                where.append((rr, q0))
        for (rr, q0), (o, l) in zip(where, _attend_blocks(blocks, low_q, low_k)):
            for u in range(per):
                o_ref[rr + dil * u, q0:q0 + jq, :] = o[u * jq:(u + 1) * jq]
                l_ref[rr + dil * u, q0:q0 + jq, :] = l[u * jq:(u + 1) * jq]
        return carry

    _loop(dil // r_unroll, body)


def _grouped_geometry(dil, n16):
    per = RES_GROUPS // dil
    jq = DIL_QB // per
    jk = min(DIL_KB, n16 * per) // per
    return per, jq, jk, n16 // jq


def _grouped_window(jb, per, jq, jk, n16):
    q0 = jq * jb
    return q0, int(np.clip(q0 - HALF_WINDOW // per, 0, n16 - jk))


def _dilated_masks(seq):
    n16 = seq // RES_GROUPS
    neg = np.float32(NEG_INF)
    q = np.arange(DIL_QB)[:, None]
    k = np.arange(DIL_KB)[None, :]
    tiles = []
    n_blocks = seq // DIL_QB
    for a in (0, 1, n_blocks - 1):
        qs = a * DIL_QB
        ks = int(np.clip(qs - HALF_WINDOW, 0, seq - DIL_KB))
        tiles.append(np.where(np.abs((k + ks) - (q + qs)) <= HALF_WINDOW, 0, neg))
    for _, dil in B_BRANCHES[1:]:
        per, jq, jk, n_jb = _grouped_geometry(dil, n16)
        pos = lambda i, n: per * (i % n) + i // n
        for jb in range(n_jb):
            q0, k0 = _grouped_window(jb, per, jq, jk, n16)
            rel = (pos(k, jk) + per * k0) - (pos(q, jq) + per * q0)
            tile = np.where(np.abs(rel) <= HALF_WINDOW, 0, neg)
            tile[:, per * jk:] = neg
            tiles.append(tile)
    return jnp.asarray(np.stack(tiles).astype(np.float32))


def _dilated_kernel(qn_ref, kn_ref, vn_ref, qg_ref, kg_ref, vg_ref, mask_ref, out_ref,
                    o1_ref, l1_ref, o2_ref, l2_ref, o3_ref, l3_ref, mix_ref):
    seq = qn_ref.shape[0]
    n16 = seq // RES_GROUPS
    n_blocks = seq // DIL_QB
    low_q, low_k = _low_half(DIL_QB), _low_half(DIL_KB)

    def body(i, carry):
        blocks, starts = [], []
        for u in range(DIL_UNROLL):
            a = i * DIL_UNROLL + u
            qs = _aligned(a * DIL_QB, DIL_QB)
            ks = _aligned(_clip(a * DIL_QB - HALF_WINDOW, 0, seq - DIL_KB), HALF_WINDOW)
            variant = _select(a == 0, 0, _select(a == n_blocks - 1, 2, 1))
            blocks.append((qn_ref[pl.ds(qs, DIL_QB), :], kn_ref[pl.ds(ks, DIL_KB), :],
                           vn_ref[pl.ds(ks, DIL_KB), :], mask_ref[variant]))
            starts.append(qs)
        for qs, (o, l) in zip(starts, _attend_blocks(blocks, low_q, low_k)):
            o1_ref[pl.ds(qs, DIL_QB), :] = o
            l1_ref[pl.ds(qs, DIL_QB), :] = l
        return carry

    _loop(n_blocks // DIL_UNROLL, body)
    base4 = 3
    base16 = base4 + _grouped_geometry(B_BRANCHES[1][1], n16)[3]
    _grouped_branch(qg_ref, kg_ref, vg_ref, mask_ref, base4, o2_ref, l2_ref, B_BRANCHES[1][1])
    _grouped_branch(qg_ref, kg_ref, vg_ref, mask_ref, base16, o3_ref, l3_ref, B_BRANCHES[2][1])
    for r in range(RES_GROUPS):
        rows = pl.ds(r, n16, stride=RES_GROUPS)
        la, lb, lc = l1_ref[rows, :], l2_ref[r], l3_ref[r]
        m = jnp.maximum(jnp.maximum(la, lb), lc)
        ea, eb, ec = jnp.exp(la - m), jnp.exp(lb - m), jnp.exp(lc - m)
        mix_ref[rows, :] = (ea * o1_ref[rows, :] + eb * o2_ref[r] + ec * o3_ref[r]) / (ea + eb + ec)
    out_ref[...] = mix_ref[...].astype(BF)


def _dilated(qn, kn, vn, qg, kg, vg, batch, seq):
    n16 = seq // RES_GROUPS
    nat = pl.BlockSpec((None, seq, LANES), lambda b, hp: (b, 0, hp))
    grp = pl.BlockSpec((None, RES_GROUPS, n16, LANES), lambda b, hp: (b, 0, 0, hp))
    nat3 = lambda t: t.reshape(batch, seq, B_WIDTH)
    masks = _dilated_masks(seq)
    out = pl.pallas_call(
        _dilated_kernel,
        grid=(batch, B_WIDTH // LANES),
        in_specs=[nat, nat, nat, grp, grp, grp, _const_spec(masks.shape)],
        out_specs=nat,
        out_shape=jax.ShapeDtypeStruct((batch, seq, B_WIDTH), BF),
        scratch_shapes=[pltpu.VMEM((seq, LANES), F32), pltpu.VMEM((seq, LANES), F32)]
        + [pltpu.VMEM((RES_GROUPS, n16, LANES), F32)] * 4 + [pltpu.VMEM((seq, LANES), F32)],
        compiler_params=_params(2),
        name="dilated_attention",
    )(nat3(qn), nat3(kn), nat3(vn), qg, kg, vg, masks)
    return out.reshape(batch * seq, B_WIDTH)


def _proj_odd_kernel(h_ref, g_ref, w_ref, o_ref):
    xn = _rms(h_ref[...], g_ref[...]).astype(BF)
    z = _dot(xn, w_ref[...])
    o_ref[:, 0:C_WIDTH] = (z[:, 0:C_WIDTH] * LOG2E).astype(BF)
    o_ref[:, C_WIDTH:] = z[:, C_WIDTH:].astype(BF)


def _proj_odd(h, g, w, tm=512):
    T = h.shape[0]
    n = w.shape[1]
    return pl.pallas_call(
        _proj_odd_kernel,
        grid=(T // tm,),
        in_specs=[pl.BlockSpec((tm, D_MODEL), lambda i: (i, 0)), _const_spec(g.shape), _const_spec(w.shape)],
        out_specs=pl.BlockSpec((tm, n), lambda i: (i, 0)),
        out_shape=jax.ShapeDtypeStruct((T, n), BF),
        compiler_params=_params(1),
        name="proj_odd",
    )(h, g, w)


def _na_kernel(q_ref, k_ref, v_ref, slab_ref, o_ref, vm_ref, b_ref, *, seq):
    rows = seq // GRID_W
    nq = NA_QROWS * GRID_W
    nk = NA_KROWS * GRID_W
    low = _low_half(nq)

    @pl.when(pl.program_id(1) == 0)
    def _():
        low_c = _low_half(GRID_W)
        index = _na_slab_index(seq)
        for v, qr, kp in np.ndindex(NA_VARIANTS, NA_QROWS, NA_KROWS // 2):
            i0, i1 = int(index[v, qr, 2 * kp]), int(index[v, qr, 2 * kp + 1])
            for j in range(2):
                b_ref[j, v, qr * GRID_W:(qr + 1) * GRID_W, kp * LANES:(kp + 1) * LANES] = jnp.where(
                    low_c, slab_ref[j, i0], slab_ref[j, i1])

    vals = _pair_values(v_ref[...], _low_half(seq))
    vm_ref[0] = vals[0]
    vm_ref[1] = vals[1]
    n_blocks = rows // NA_QROWS

    def body(i, carry):
        scores, where = [], []
        for u in range(NA_UNROLL):
            a = i * NA_UNROLL + u
            qs = _aligned(a * nq, nq)
            ks = _aligned(_clip(NA_QROWS * a - NA_ROWS // 2, 0, rows - NA_KROWS) * GRID_W, GRID_W)
            var = _select(a < 2, a, _select(a >= n_blocks - 2, a - (n_blocks - NA_VARIANTS), 2))
            q2 = q_ref[pl.ds(qs, nq), :]
            k2 = k_ref[pl.ds(ks, nk), :]
            zero = jnp.zeros_like(q2)
            scores.append([_dot_nt(jnp.where(low, q2, zero), k2) + b_ref[0, var],
                           _dot_nt(jnp.where(low, zero, q2), k2) + b_ref[1, var]])
            where.append((qs, ks))
        probs = [[_exp_rows(s)[0] for s in pair] for pair in scores]
        for (qs, ks), pair in zip(where, probs):
            o0 = _dot(pair[0], vm_ref[0, pl.ds(ks, nk), :])
            o1 = _dot(pair[1], vm_ref[1, pl.ds(ks, nk), :])
            o_ref[pl.ds(qs, nq), :] = _pair_finish(o0, o1, low).astype(BF)
        return carry

    _loop(n_blocks // NA_UNROLL, body)


def _na_slab_index(seq):
    rows = seq // GRID_W
    n_blocks = rows // NA_QROWS
    n_off = 2 * NA_ROWS - 1
    idx = np.full((NA_VARIANTS, NA_QROWS, NA_KROWS), n_off, np.int32)
    for v, a in enumerate((0, 1, 2, n_blocks - 2, n_blocks - 1)):
        ws = int(np.clip(NA_QROWS * a - NA_ROWS // 2, 0, rows - NA_KROWS))
        for qr in range(NA_QROWS):
            r = NA_QROWS * a + qr
            rs = int(np.clip(r - NA_ROWS // 2, 0, rows - NA_ROWS))
            for kr in range(NA_KROWS):
                krow = ws + kr
                if rs <= krow < rs + NA_ROWS:
                    idx[v, qr, kr] = krow - r + NA_ROWS - 1
    return idx


def _na_bias_slabs(rpb):
    col = np.arange(GRID_W)
    cs = np.clip(col - NA_COLS // 2, 0, GRID_W - NA_COLS)
    col_valid = (col[None, :] >= cs[:, None]) & (col[None, :] < cs[:, None] + NA_COLS)
    col_off = np.clip(col[None, :] - col[:, None] + NA_COLS - 1, 0, 2 * NA_COLS - 2)
    t2 = jnp.where(col_valid[None, None], rpb[:, :, col_off] * LOG2E, NEG_INF)
    t2 = jnp.concatenate([t2, jnp.full_like(t2[:, :1], NEG_INF)], axis=1)
    return jnp.concatenate([t2, t2], axis=-1)


def _neighbourhood(qkv, slabs, batch, seq):
    qkv3 = qkv.reshape(batch, seq, 3 * C_WIDTH)
    n_hp = C_WIDTH // LANES
    blk = lambda off: pl.BlockSpec((None, seq, LANES), lambda hp, b: (b, 0, off + hp))
    out = pl.pallas_call(
        functools.partial(_na_kernel, seq=seq),
        grid=(n_hp, batch),
        in_specs=[blk(0), blk(n_hp), blk(2 * n_hp),
                  pl.BlockSpec((2,) + slabs.shape[1:], lambda hp, b: (hp, 0, 0, 0))],
        out_specs=pl.BlockSpec((None, seq, LANES), lambda hp, b: (b, 0, hp)),
        out_shape=jax.ShapeDtypeStruct((batch, seq, C_WIDTH), BF),
        scratch_shapes=[pltpu.VMEM((2, seq, LANES), BF),
                        pltpu.VMEM((2, NA_VARIANTS, NA_QROWS * GRID_W, NA_KROWS * GRID_W), F32)],
        compiler_params=_params(2),
        name="neighbourhood_attention",
    )(qkv3, qkv3, qkv3, slabs)
    return out.reshape(batch * seq, C_WIDTH)


def _post_kernel(*refs, n_attn, final):
    h_ref = refs[0]
    attn = refs[1:1 + n_attn]
    wo_ref, g_ref, wg_ref, wu_ref, wd_ref = refs[1 + n_attn:6 + n_attn]
    fg_ref = refs[6 + n_attn] if final else None
    out_ref = refs[-1]
    h1 = h_ref[...]
    off = 0
    for a in attn:
        width = a.shape[1]
        h1 = h1 + _dot(a[...], wo_ref[off:off + width, :])
        off += width
    xn = _rms(h1, g_ref[...]).astype(BF)
    gate = _dot(xn, wg_ref[...])
    up = _dot(xn, wu_ref[...])
    act = (gate * (1.0 / (1.0 + jnp.exp(-gate))) * up).astype(BF)
    y = h1 + _dot(act, wd_ref[...])
    if final:
        y = _rms(y, fg_ref[...])
    out_ref[...] = y


def _post(h, attn, wo, g, wg, wu, wd, final_g=None, tm=512):
    T = h.shape[0]
    final = final_g is not None
    row = lambda c: pl.BlockSpec((tm, c), lambda i: (i, 0))
    consts = [wo, g, wg, wu, wd] + ([final_g] if final else [])
    return pl.pallas_call(
        functools.partial(_post_kernel, n_attn=len(attn), final=final),
        grid=(T // tm,),
        in_specs=[row(D_MODEL)] + [row(a.shape[1]) for a in attn] + [_const_spec(c.shape) for c in consts],
        out_specs=row(D_MODEL),
        out_shape=jax.ShapeDtypeStruct((T, D_MODEL), F32),
        compiler_params=_params(1),
        name="outproj_ffn",
    )(h, *attn, *consts)


def _rope_tables(seq):
    pos = jnp.arange(seq, dtype=F32)

    def cs(dim):
        inv = ROPE_THETA ** (-jnp.arange(0, dim, 2, dtype=F32) / dim)
        ang = pos[:, None] * inv[None, :]
        return jnp.cos(ang), jnp.sin(ang)

    cos_a, sin_a = cs(A_ROPE_DIM)
    cos2, sin2 = jnp.tile(cos_a, (1, 2)), jnp.tile(sin_a, (1, 2))
    zeros = lambda n: jnp.zeros((seq, n), F32)
    scale = (A_NOPE_DIM + A_ROPE_DIM) ** -0.5 * LOG2E
    pad = LANES - A_NOPE_DIM - A_ROPE_DIM
    cq = jnp.concatenate([jnp.full((seq, A_NOPE_DIM), scale, F32), cos2 * scale, zeros(pad)], axis=1)
    sq = jnp.concatenate([zeros(A_NOPE_DIM), sin2 * scale, zeros(pad)], axis=1)
    ck = jnp.concatenate([cos2, zeros(LANES - A_ROPE_DIM)], axis=1)
    sk = jnp.concatenate([sin2, zeros(LANES - A_ROPE_DIM)], axis=1)
    ta = jnp.concatenate([cq, sq, ck, sk], axis=1)
    cos_b, sin_b = cs(HEAD_DIM)
    zb = jnp.zeros_like(sin_b)
    c = jnp.tile(cos_b, (1, 4))
    s_hi = jnp.tile(jnp.concatenate([zb, sin_b], axis=1), (1, 2))
    s_lo = jnp.tile(jnp.concatenate([-sin_b, zb], axis=1), (1, 2))
    tb = jnp.concatenate([c, s_hi, s_lo], axis=1)
    tb = jnp.concatenate([tb, tb * LOG2E], axis=1)
    return ta, tb


def _rot_half_cols(w):
    half = w.shape[1] // 2
    return jnp.concatenate([-w[:, half:], w[:, :half]], axis=1)


def _prep_even(w_in, q_norm, w_q_up, kv_norm, w_kv_up, w_out):
    d = w_in.shape[0]
    c_q, c_kv, k_pe, qkv_b = jnp.split(w_in, [A_Q_RANK, COL_QB, COL_QB + A_ROPE_DIM], axis=1)
    q_b, k_b, v_b = jnp.split(qkv_b, 3, axis=1)
    padc = jnp.zeros((d, LANES - A_ROPE_DIM), F32)
    win = jnp.concatenate([c_q, c_kv, q_b * HEAD_DIM ** -0.5, k_b, v_b,
                           k_pe, padc, _rot_half_cols(k_pe), padc], axis=1).astype(BF)
    wq3 = w_q_up.reshape(A_Q_RANK, A_HEADS, A_NOPE_DIM + A_ROPE_DIM)
    nope, rope = wq3[..., :A_NOPE_DIM], wq3[..., A_NOPE_DIM:]
    zpad = jnp.zeros((A_Q_RANK, A_HEADS, LANES - A_NOPE_DIM - A_ROPE_DIM), F32)
    wq = jnp.concatenate([nope, rope, zpad], axis=-1).reshape(A_Q_RANK, A_PAD).astype(BF)
    rope_rot = jnp.concatenate([-rope[..., A_ROPE_DIM // 2:], rope[..., :A_ROPE_DIM // 2]], axis=-1)
    wqr = jnp.concatenate([jnp.zeros_like(nope), rope_rot, zpad], axis=-1).reshape(A_Q_RANK, A_PAD).astype(BF)
    wkv3 = w_kv_up.reshape(A_KV_RANK, A_HEADS, A_NOPE_DIM + A_V_DIM)
    k_nope, v = wkv3[..., :A_NOPE_DIM], wkv3[..., A_NOPE_DIM:]
    zk = jnp.zeros_like(k_nope)
    wk = jnp.concatenate([k_nope, zk], axis=-1).reshape(A_KV_RANK, A_PAD).astype(BF)
    v_even = jnp.concatenate([v, jnp.zeros_like(v)], axis=-1)
    v_odd = jnp.concatenate([jnp.zeros_like(v), v], axis=-1)
    odd = (np.arange(A_HEADS) % 2 == 1)[None, :, None]
    wv = jnp.where(odd, v_odd, v_even).reshape(A_KV_RANK, A_PAD).astype(BF)
    e = np.zeros((LANES, A_PAD), np.float32)
    for hd in range(A_HEADS):
        for j in range(A_ROPE_DIM):
            e[j, hd * LANES + A_NOPE_DIM + j] = 1.0
    return dict(win=win, qg=q_norm[None, :], kvg=kv_norm[None, :], wq=wq, wqr=wqr, wk=wk, wv=wv,
                e=jnp.asarray(e, BF), wo=w_out.astype(BF))


def kernel(x, ev_norm, ev_w_in, ev_q_norm, ev_w_q_up, ev_kv_norm, ev_w_kv_up, ev_w_out, od_norm, od_w_qkv, od_rpb, od_w_out, ffn_norm, ffn_w_gate, ffn_w_up, ffn_w_down, final_norm):
    batch, seq, d = x.shape
    h = x.reshape(batch * seq, d)
    ta, tb = _rope_tables(seq)
    for layer in range(DEPTH):
        i = layer // 2
        if layer % 2 == 0:
            w = _prep_even(ev_w_in[i], ev_q_norm[i], ev_w_q_up[i], ev_kv_norm[i], ev_w_kv_up[i], ev_w_out[i])
            qa, ka, va, qb, kb, vb, qg, kg, vg = _proj_even(h, ev_norm[i][None, :], w, ta, tb, seq)
            o_a = _mla(qa, ka, va, batch, seq)
            o_b = _dilated(qb, kb, vb, qg, kg, vg, batch, seq)
            attn, wo = [o_a, o_b], w["wo"]
        else:
            wqkv = jnp.concatenate([od_w_qkv[i][:, :C_WIDTH] * HEAD_DIM ** -0.5, od_w_qkv[i][:, C_WIDTH:]],
                                   axis=1).astype(BF)
            qkv = _proj_odd(h, od_norm[i][None, :], wqkv)
            attn = [_neighbourhood(qkv, _na_bias_slabs(od_rpb[i]), batch, seq)]
            wo = od_w_out[i].astype(BF)
        h = _post(h, attn, wo, ffn_norm[layer][None, :], ffn_w_gate[layer].astype(BF),
                  ffn_w_up[layer].astype(BF), ffn_w_down[layer].astype(BF),
                  final_g=final_norm[None, :] if layer == DEPTH - 1 else None)
    return h.reshape(batch, seq, d)
```

```python
import functools

import numpy as np
import jax
import jax.numpy as jnp
from jax import lax
from jax.experimental import pallas as pl
from jax.experimental.pallas import tpu as pltpu

BF = jnp.bfloat16
F32 = jnp.float32

D_MODEL = 1024
DEPTH = 4
HEAD_DIM = 64
A_HEADS = 8
A_Q_RANK = 256
A_KV_RANK = 128
A_NOPE_DIM = 64
A_ROPE_DIM = 32
A_V_DIM = 64
B_HEADS = 8
B_BRANCHES = ((128, 1), (512, 4), (2048, 16))
C_HEADS = 16
GRID_W = 64
NA_ROWS = 8
NA_COLS = 16
D_FF = -(-8 * D_MODEL // (3 * 256)) * 256
ROPE_THETA = 10000.0
EPS = 1e-6
NEG_INF = -1e30
LOG2E = float(np.log2(np.e))
LN2 = float(np.log(2.0))

LANES = 128
B_WIDTH = B_HEADS * HEAD_DIM
C_WIDTH = C_HEADS * HEAD_DIM
A_PAD = A_HEADS * LANES
IN_COLS = A_Q_RANK + A_KV_RANK + 3 * B_WIDTH + 2 * LANES
COL_QB = A_Q_RANK + A_KV_RANK
COL_KB = COL_QB + B_WIDTH
COL_VB = COL_KB + B_WIDTH
COL_KPE = COL_VB + B_WIDTH
COL_KPE_ROT = COL_KPE + LANES
HALF_WINDOW = 64
NA_QROWS = 2
NA_KROWS = 10
NA_VARIANTS = 5
NA_UNROLL = 16
RES_GROUPS = 16
DEINT = 4
DIL_QB = 128
DIL_KB = 256
DIL_UNROLL = 16
MLA_GROUP = 2
VMEM_LIMIT = 56 * 1024 * 1024


def _rms(x, g):
    return x * lax.rsqrt(jnp.mean(x * x, axis=-1, keepdims=True) + EPS) * g


def _dot(a, b):
    return jnp.dot(a, b, preferred_element_type=F32)


def _dot_nt(a, b):
    return lax.dot_general(a, b, (((1,), (1,)), ((), ())), preferred_element_type=F32)


def _loop(trips, body):
    if trips == 1:
        body(0, 0)
    else:
        lax.fori_loop(0, trips, body, 0)


def _clip(x, lo, hi):
    return min(max(x, lo), hi) if isinstance(x, int) else jnp.clip(x, lo, hi)


def _select(pred, a, b):
    return (a if pred else b) if isinstance(pred, bool) else jnp.where(pred, a, b)


def _aligned(x, m):
    return x if isinstance(x, int) else pl.multiple_of(x, m)


def _const_spec(shape):
    nd = len(shape)
    return pl.BlockSpec(shape, lambda *_: (0,) * nd, pipeline_mode=pl.Buffered(1))


def _params(n_grid):
    return pltpu.CompilerParams(dimension_semantics=("arbitrary",) * n_grid,
                                vmem_limit_bytes=VMEM_LIMIT)


def _proj_even_kernel(h_ref, g_ref, win_ref, qg_ref, kvg_ref, wq_ref, wqr_ref, wkv_ref,
                      ta_ref, tb_ref, qa_ref, ka_ref, va_ref, qb_ref, kb_ref, vb_ref,
                      qr_ref, kr_ref, vr_ref, stage_ref, mid_ref):
    xn = _rms(h_ref[...], g_ref[...]).astype(BF)
    z = _dot(xn, win_ref[...])
    cq = _rms(z[:, 0:A_Q_RANK], qg_ref[...]).astype(BF)
    ckv = _rms(z[:, A_Q_RANK:COL_QB], kvg_ref[...]).astype(BF)
    qa = _dot(cq, wq_ref[...])
    qr = _dot(cq, wqr_ref[...])
    cq_t, sq_t = ta_ref[:, 0:LANES], ta_ref[:, LANES:2 * LANES]
    ck_t, sk_t = ta_ref[:, 2 * LANES:3 * LANES], ta_ref[:, 3 * LANES:4 * LANES]
    for hd in range(A_HEADS):
        sl = slice(hd * LANES, (hd + 1) * LANES)
        qa_ref[:, sl] = (qa[:, sl] * cq_t + qr[:, sl] * sq_t).astype(BF)
    kpe = z[:, COL_KPE:COL_KPE + LANES] * ck_t + z[:, COL_KPE_ROT:COL_KPE_ROT + LANES] * sk_t
    kv = _dot(jnp.concatenate([ckv, kpe.astype(BF)], axis=1), wkv_ref[...])
    ka_ref[...] = kv[:, 0:A_PAD].astype(BF)
    col = lax.broadcasted_iota(jnp.int32, (1, A_PAD), 1)
    own_half = ((col // HEAD_DIM) % 2) == ((col // LANES) % 2)
    va_ref[...] = (kv[:, A_PAD:] + jnp.where(own_half, 0.0, 1.0)).astype(BF)
    for p in range(B_WIDTH // LANES):
        for idx, (col, out) in enumerate(((COL_QB, qb_ref), (COL_KB, kb_ref))):
            t0 = (1 - idx) * 3 * LANES
            c_t, s_hi, s_lo = (tb_ref[:, t0 + n * LANES:t0 + (n + 1) * LANES] for n in range(3))
            xs = z[:, col + p * LANES: col + (p + 1) * LANES]
            r = (xs * c_t + pltpu.roll(xs, HEAD_DIM // 2, 1) * s_hi
                 + pltpu.roll(xs, LANES - HEAD_DIM // 2, 1) * s_lo)
            out[:, p * LANES:(p + 1) * LANES] = r.astype(BF)
            stage_ref[idx, p] = r
        stage_ref[2, p] = z[:, COL_VB + p * LANES:COL_VB + (p + 1) * LANES]
    vb_ref[...] = z[:, COL_VB:COL_KPE].astype(BF)
    tm = stage_ref.shape[2]
    for idx, out in enumerate((qr_ref, kr_ref, vr_ref)):
        for p in range(B_WIDTH // LANES):
            for b in range(DEINT):
                mid_ref[idx, p, b] = stage_ref[idx, p, pl.ds(b, tm // DEINT, stride=DEINT), :]
            for r in range(RES_GROUPS):
                out[r, :, p * LANES:(p + 1) * LANES] = (
                    mid_ref[idx, p, r % DEINT, pl.ds(r // DEINT, tm // RES_GROUPS, stride=DEINT), :].astype(BF))


def _proj_even(h, g, w, ta, tb, seq, tm=512):
    T = h.shape[0]
    nseq = seq // tm
    row = lambda c: pl.BlockSpec((tm, c), lambda i: (i, 0))
    tab = lambda c: pl.BlockSpec((tm, c), lambda i: (i % nseq, 0))
    consts = [g, w["win"], w["qg"], w["kvg"], w["wq"], w["wqr"], w["wkv"]]
    out_cols = (A_PAD, A_PAD, A_PAD, B_WIDTH, B_WIDTH, B_WIDTH)
    grouped = pl.BlockSpec((None, RES_GROUPS, tm // RES_GROUPS, B_WIDTH), lambda i: (i // nseq, 0, i % nseq, 0))
    grouped_shape = jax.ShapeDtypeStruct((T // seq, RES_GROUPS, seq // RES_GROUPS, B_WIDTH), BF)
    return pl.pallas_call(
        _proj_even_kernel,
        grid=(T // tm,),
        in_specs=[row(D_MODEL)] + [_const_spec(c.shape) for c in consts] + [tab(4 * LANES), tab(6 * LANES)],
        out_specs=[row(c) for c in out_cols] + [grouped] * 3,
        out_shape=[jax.ShapeDtypeStruct((T, c), BF) for c in out_cols] + [grouped_shape] * 3,
        scratch_shapes=[pltpu.VMEM((3, B_WIDTH // LANES, tm, LANES), F32),
                        pltpu.VMEM((3, B_WIDTH // LANES, DEINT, tm // DEINT, LANES), F32)],
        compiler_params=_params(1),
        name="proj_even",
    )(h, *consts, ta, tb)


def _mla_kernel(q_ref, k_ref, v_ref, o_ref):
    low = _low_half(q_ref.shape[0])
    tile = lambda hd: slice(hd * LANES, (hd + 1) * LANES)
    for g in range(A_HEADS // MLA_GROUP):
        heads = range(g * MLA_GROUP, (g + 1) * MLA_GROUP)
        scores = [_dot_nt(q_ref[:, tile(hd)], k_ref[:, tile(hd)]) for hd in heads]
        probs = [_exp_rows(s)[0] for s in scores]
        outs = [_dot(p, v_ref[:, tile(hd)]) for p, hd in zip(probs, heads)]
        for pair in range(MLA_GROUP // 2):
            o_ref[:, tile(g * MLA_GROUP // 2 + pair)] = _pair_finish(outs[2 * pair], outs[2 * pair + 1], low).astype(BF)


def _mla(qa, ka, va, batch, seq, tq=1024):
    q3, k3, v3 = (t.reshape(batch, seq, A_PAD) for t in (qa, ka, va))
    out = pl.pallas_call(
        _mla_kernel,
        grid=(batch, seq // tq),
        in_specs=[pl.BlockSpec((None, tq, A_PAD), lambda b, i: (b, i, 0)),
                  pl.BlockSpec((None, seq, A_PAD), lambda b, i: (b, 0, 0)),
                  pl.BlockSpec((None, seq, A_PAD), lambda b, i: (b, 0, 0))],
        out_specs=pl.BlockSpec((None, tq, B_WIDTH), lambda b, i: (b, i, 0)),
        out_shape=jax.ShapeDtypeStruct((batch, seq, A_HEADS * A_V_DIM), BF),
        compiler_params=_params(2),
        name="mla_attention",
    )(q3, k3, v3)
    return out.reshape(batch * seq, A_HEADS * A_V_DIM)


def _low_half(n):
    return lax.broadcasted_iota(jnp.int32, (n, LANES), 1) < HEAD_DIM


def _pair_scores(q2, k2, bias, low_q):
    zero = jnp.zeros_like(q2)
    return [_dot_nt(jnp.where(low_q, q2, zero), k2) + bias, _dot_nt(jnp.where(low_q, zero, q2), k2) + bias]


def _pair_values(v2, low_k):
    one = jnp.ones_like(v2)
    return [jnp.where(low_k, v2, one), jnp.where(low_k, one, v2)]


def _exp_rows(s):
    m = jnp.max(s, axis=-1, keepdims=True)
    return jnp.exp2(s - m).astype(BF), m


def _pair_finish(o0, o1, low_q, m0=None, m1=None):
    den = pltpu.roll(jnp.where(low_q, o1, o0), HEAD_DIM, 1)
    out = jnp.where(low_q, o0, o1) / den
    if m0 is None:
        return out
    return out, jnp.where(low_q, m0, m1) * LN2 + jnp.log(den)


def _attend_blocks(blocks, low_q, low_k):
    scores = [_pair_scores(q2, k2, bias, low_q) for q2, k2, _, bias in blocks]
    probs = [[_exp_rows(s) for s in pair] for pair in scores]
    results = []
    for (_, _, v2, _), pair in zip(blocks, probs):
        vals = _pair_values(v2, low_k)
        o0, o1 = _dot(pair[0][0], vals[0]), _dot(pair[1][0], vals[1])
        results.append(_pair_finish(o0, o1, low_q, pair[0][1], pair[1][1]))
    return results


def _grouped_branch(qg_ref, kg_ref, vg_ref, mask_ref, mask_base, o_ref, l_ref, dil):
    n16 = qg_ref.shape[1]
    per, jq, jk, n_jb = _grouped_geometry(dil, n16)
    nq, nk = per * jq, per * jk
    low_q, low_k = _low_half(nq), _low_half(nk)
    r_unroll = max(1, DIL_UNROLL // n_jb)

    def body(i, carry):
        blocks, where = [], []
        for ru in range(r_unroll):
            rr = i * r_unroll + ru
            for jb in range(n_jb):
                q0, k0 = _grouped_window(jb, per, jq, jk, n16)
                cat = lambda ref, lo, n: jnp.concatenate(
                    [ref[rr + dil * u, lo:lo + n, :] for u in range(per)], axis=0)
                blocks.append((cat(qg_ref, q0, jq), cat(kg_ref, k0, jk), cat(vg_ref, k0, jk),
                               mask_ref[mask_base + jb, :, 0:nk]))
                where.append((rr, q0))
        for (rr, q0), (o, l) in zip(where, _attend_blocks(blocks, low_q, low_k)):
            for u in range(per):
                o_ref[rr + dil * u, q0:q0 + jq, :] = o[u * jq:(u + 1) * jq]
                l_ref[rr + dil * u, q0:q0 + jq, :] = l[u * jq:(u + 1) * jq]
        return carry

    _loop(dil // r_unroll, body)


def _grouped_geometry(dil, n16):
    per = RES_GROUPS // dil
    jq = DIL_QB // per
    jk = min(DIL_KB, n16 * per) // per
    return per, jq, jk, n16 // jq


def _grouped_window(jb, per, jq, jk, n16):
    q0 = jq * jb
    return q0, int(np.clip(q0 - HALF_WINDOW // per, 0, n16 - jk))


def _dilated_masks(seq):
    n16 = seq // RES_GROUPS
    neg = np.float32(NEG_INF)
    q = np.arange(DIL_QB)[:, None]
    k = np.arange(DIL_KB)[None, :]
    tiles = []
    n_blocks = seq // DIL_QB
    for a in (0, 1, n_blocks - 1):
        qs = a * DIL_QB
        ks = int(np.clip(qs - HALF_WINDOW, 0, seq - DIL_KB))
        tiles.append(np.where(np.abs((k + ks) - (q + qs)) <= HALF_WINDOW, 0, neg))
    for _, dil in B_BRANCHES[1:]:
        per, jq, jk, n_jb = _grouped_geometry(dil, n16)
        pos = lambda i, n: per * (i % n) + i // n
        for jb in range(n_jb):
            q0, k0 = _grouped_window(jb, per, jq, jk, n16)
            rel = (pos(k, jk) + per * k0) - (pos(q, jq) + per * q0)
            tile = np.where(np.abs(rel) <= HALF_WINDOW, 0, neg)
            tile[:, per * jk:] = neg
            tiles.append(tile)
    return jnp.asarray(np.stack(tiles).astype(np.float32))


def _dilated_kernel(qn_ref, kn_ref, vn_ref, qg_ref, kg_ref, vg_ref, mask_ref, out_ref,
                    o1_ref, l1_ref, o2_ref, l2_ref, o3_ref, l3_ref, mix_ref, om_ref, lm_ref, mm_ref):
    seq = qn_ref.shape[0]
    n16 = seq // RES_GROUPS
    n_blocks = seq // DIL_QB
    low_q, low_k = _low_half(DIL_QB), _low_half(DIL_KB)

    def body(i, carry):
        blocks, starts = [], []
        for u in range(DIL_UNROLL):
            a = i * DIL_UNROLL + u
            qs = _aligned(a * DIL_QB, DIL_QB)
            ks = _aligned(_clip(a * DIL_QB - HALF_WINDOW, 0, seq - DIL_KB), HALF_WINDOW)
            variant = _select(a == 0, 0, _select(a == n_blocks - 1, 2, 1))
            blocks.append((qn_ref[pl.ds(qs, DIL_QB), :], kn_ref[pl.ds(ks, DIL_KB), :],
                           vn_ref[pl.ds(ks, DIL_KB), :], mask_ref[variant]))
            starts.append(qs)
        for qs, (o, l) in zip(starts, _attend_blocks(blocks, low_q, low_k)):
            o1_ref[pl.ds(qs, DIL_QB), :] = o
            l1_ref[pl.ds(qs, DIL_QB), :] = l
        return carry

    _loop(n_blocks // DIL_UNROLL, body)
    base4 = 3
    base16 = base4 + _grouped_geometry(B_BRANCHES[1][1], n16)[3]
    _grouped_branch(qg_ref, kg_ref, vg_ref, mask_ref, base4, o2_ref, l2_ref, B_BRANCHES[1][1])
    _grouped_branch(qg_ref, kg_ref, vg_ref, mask_ref, base16, o3_ref, l3_ref, B_BRANCHES[2][1])
    for b in range(DEINT):
        quarter = pl.ds(b, seq // DEINT, stride=DEINT)
        om_ref[b] = o1_ref[quarter, :]
        lm_ref[b] = l1_ref[quarter, :]
    for r in range(RES_GROUPS):
        rows = pl.ds(r // DEINT, n16, stride=DEINT)
        la, lb, lc = lm_ref[r % DEINT, rows, :], l2_ref[r], l3_ref[r]
        m = jnp.maximum(jnp.maximum(la, lb), lc)
        ea, eb, ec = jnp.exp(la - m), jnp.exp(lb - m), jnp.exp(lc - m)
        mm_ref[r % DEINT, rows, :] = (ea * om_ref[r % DEINT, rows, :] + eb * o2_ref[r] + ec * o3_ref[r]) / (ea + eb + ec)
    for b in range(DEINT):
        mix_ref[pl.ds(b, seq // DEINT, stride=DEINT), :] = mm_ref[b]
    out_ref[...] = mix_ref[...].astype(BF)


def _dilated(qn, kn, vn, qg, kg, vg, batch, seq):
    n16 = seq // RES_GROUPS
    nat = pl.BlockSpec((None, seq, LANES), lambda b, hp: (b, 0, hp))
    grp = pl.BlockSpec((None, RES_GROUPS, n16, LANES), lambda b, hp: (b, 0, 0, hp))
    nat3 = lambda t: t.reshape(batch, seq, B_WIDTH)
    masks = _dilated_masks(seq)
    out = pl.pallas_call(
        _dilated_kernel,
        grid=(batch, B_WIDTH // LANES),
        in_specs=[nat, nat, nat, grp, grp, grp, _const_spec(masks.shape)],
        out_specs=nat,
        out_shape=jax.ShapeDtypeStruct((batch, seq, B_WIDTH), BF),
        scratch_shapes=[pltpu.VMEM((seq, LANES), F32), pltpu.VMEM((seq, LANES), F32)]
        + [pltpu.VMEM((RES_GROUPS, n16, LANES), F32)] * 4 + [pltpu.VMEM((seq, LANES), F32)]
        + [pltpu.VMEM((DEINT, seq // DEINT, LANES), F32)] * 3,
        compiler_params=_params(2),
        name="dilated_attention",
    )(nat3(qn), nat3(kn), nat3(vn), qg, kg, vg, masks)
    return out.reshape(batch * seq, B_WIDTH)


def _proj_odd_kernel(h_ref, g_ref, w_ref, o_ref):
    xn = _rms(h_ref[...], g_ref[...]).astype(BF)
    z = _dot(xn, w_ref[...])
    o_ref[:, 0:C_WIDTH] = (z[:, 0:C_WIDTH] * LOG2E).astype(BF)
    o_ref[:, C_WIDTH:] = z[:, C_WIDTH:].astype(BF)


def _proj_odd(h, g, w, tm=512):
    T = h.shape[0]
    n = w.shape[1]
    return pl.pallas_call(
        _proj_odd_kernel,
        grid=(T // tm,),
        in_specs=[pl.BlockSpec((tm, D_MODEL), lambda i: (i, 0)), _const_spec(g.shape), _const_spec(w.shape)],
        out_specs=pl.BlockSpec((tm, n), lambda i: (i, 0)),
        out_shape=jax.ShapeDtypeStruct((T, n), BF),
        compiler_params=_params(1),
        name="proj_odd",
    )(h, g, w)


def _na_kernel(q_ref, k_ref, v_ref, slab_ref, o_ref, vm_ref, b_ref, *, seq):
    rows = seq // GRID_W
    nq = NA_QROWS * GRID_W
    nk = NA_KROWS * GRID_W
    low = _low_half(nq)

    @pl.when(pl.program_id(1) == 0)
    def _():
        low_c = _low_half(GRID_W)
        index = _na_slab_index(seq)
        for v, qr, kp in np.ndindex(NA_VARIANTS, NA_QROWS, NA_KROWS // 2):
            i0, i1 = int(index[v, qr, 2 * kp]), int(index[v, qr, 2 * kp + 1])
            for j in range(2):
                b_ref[j, v, qr * GRID_W:(qr + 1) * GRID_W, kp * LANES:(kp + 1) * LANES] = jnp.where(
                    low_c, slab_ref[j, i0], slab_ref[j, i1])

    vals = _pair_values(v_ref[...], _low_half(seq))
    vm_ref[0] = vals[0]
    vm_ref[1] = vals[1]
    n_blocks = rows // NA_QROWS

    def body(i, carry):
        scores, where = [], []
        for u in range(NA_UNROLL):
            a = i * NA_UNROLL + u
            qs = _aligned(a * nq, nq)
            ks = _aligned(_clip(NA_QROWS * a - NA_ROWS // 2, 0, rows - NA_KROWS) * GRID_W, GRID_W)
            var = _select(a < 2, a, _select(a >= n_blocks - 2, a - (n_blocks - NA_VARIANTS), 2))
            q2 = q_ref[pl.ds(qs, nq), :]
            k2 = k_ref[pl.ds(ks, nk), :]
            zero = jnp.zeros_like(q2)
            scores.append([_dot_nt(jnp.where(low, q2, zero), k2) + b_ref[0, var],
                           _dot_nt(jnp.where(low, zero, q2), k2) + b_ref[1, var]])
            where.append((qs, ks))
        probs = [[_exp_rows(s)[0] for s in pair] for pair in scores]
        for (qs, ks), pair in zip(where, probs):
            o0 = _dot(pair[0], vm_ref[0, pl.ds(ks, nk), :])
            o1 = _dot(pair[1], vm_ref[1, pl.ds(ks, nk), :])
            o_ref[pl.ds(qs, nq), :] = _pair_finish(o0, o1, low).astype(BF)
        return carry

    _loop(n_blocks // NA_UNROLL, body)


def _na_slab_index(seq):
    rows = seq // GRID_W
    n_blocks = rows // NA_QROWS
    n_off = 2 * NA_ROWS - 1
    idx = np.full((NA_VARIANTS, NA_QROWS, NA_KROWS), n_off, np.int32)
    for v, a in enumerate((0, 1, 2, n_blocks - 2, n_blocks - 1)):
        ws = int(np.clip(NA_QROWS * a - NA_ROWS // 2, 0, rows - NA_KROWS))
        for qr in range(NA_QROWS):
            r = NA_QROWS * a + qr
            rs = int(np.clip(r - NA_ROWS // 2, 0, rows - NA_ROWS))
            for kr in range(NA_KROWS):
                krow = ws + kr
                if rs <= krow < rs + NA_ROWS:
                    idx[v, qr, kr] = krow - r + NA_ROWS - 1
    return idx


def _na_bias_slabs(rpb):
    col = np.arange(GRID_W)
    cs = np.clip(col - NA_COLS // 2, 0, GRID_W - NA_COLS)
    col_valid = (col[None, :] >= cs[:, None]) & (col[None, :] < cs[:, None] + NA_COLS)
    col_off = np.clip(col[None, :] - col[:, None] + NA_COLS - 1, 0, 2 * NA_COLS - 2)
    onehot = (col_off[..., None] == np.arange(2 * NA_COLS - 1)).astype(np.float32)
    toeplitz = jnp.einsum("hrd,ckd->hrck", rpb, onehot, precision=lax.Precision.HIGHEST)
    t2 = jnp.where(col_valid[None, None], toeplitz * LOG2E, NEG_INF)
    t2 = jnp.concatenate([t2, jnp.full_like(t2[:, :1], NEG_INF)], axis=1)
    return jnp.concatenate([t2, t2], axis=-1)


def _neighbourhood(qkv, slabs, batch, seq):
    qkv3 = qkv.reshape(batch, seq, 3 * C_WIDTH)
    n_hp = C_WIDTH // LANES
    blk = lambda off: pl.BlockSpec((None, seq, LANES), lambda hp, b: (b, 0, off + hp))
    out = pl.pallas_call(
        functools.partial(_na_kernel, seq=seq),
        grid=(n_hp, batch),
        in_specs=[blk(0), blk(n_hp), blk(2 * n_hp),
                  pl.BlockSpec((2,) + slabs.shape[1:], lambda hp, b: (hp, 0, 0, 0))],
        out_specs=pl.BlockSpec((None, seq, LANES), lambda hp, b: (b, 0, hp)),
        out_shape=jax.ShapeDtypeStruct((batch, seq, C_WIDTH), BF),
        scratch_shapes=[pltpu.VMEM((2, seq, LANES), BF),
                        pltpu.VMEM((2, NA_VARIANTS, NA_QROWS * GRID_W, NA_KROWS * GRID_W), F32)],
        compiler_params=_params(2),
        name="neighbourhood_attention",
    )(qkv3, qkv3, qkv3, slabs)
    return out.reshape(batch * seq, C_WIDTH)


def _post_kernel(*refs, n_attn, final):
    h_ref = refs[0]
    attn = refs[1:1 + n_attn]
    wo_ref, g_ref, wg_ref, wu_ref, wd_ref = refs[1 + n_attn:6 + n_attn]
    fg_ref = refs[6 + n_attn] if final else None
    out_ref = refs[-1]
    h1 = h_ref[...]
    off = 0
    for a in attn:
        width = a.shape[1]
        h1 = h1 + _dot(a[...], wo_ref[off:off + width, :])
        off += width
    xn = _rms(h1, g_ref[...]).astype(BF)
    gate = _dot(xn, wg_ref[...])
    up = _dot(xn, wu_ref[...])
    act = (gate * (1.0 / (1.0 + jnp.exp(-gate))) * up).astype(BF)
    y = h1 + _dot(act, wd_ref[...])
    if final:
        y = _rms(y, fg_ref[...])
    out_ref[...] = y


def _post(h, attn, wo, g, wg, wu, wd, final_g=None, tm=512):
    T = h.shape[0]
    final = final_g is not None
    row = lambda c: pl.BlockSpec((tm, c), lambda i: (i, 0))
    consts = [wo, g, wg, wu, wd] + ([final_g] if final else [])
    return pl.pallas_call(
        functools.partial(_post_kernel, n_attn=len(attn), final=final),
        grid=(T // tm,),
        in_specs=[row(D_MODEL)] + [row(a.shape[1]) for a in attn] + [_const_spec(c.shape) for c in consts],
        out_specs=row(D_MODEL),
        out_shape=jax.ShapeDtypeStruct((T, D_MODEL), F32),
        compiler_params=_params(1),
        name="outproj_ffn",
    )(h, *attn, *consts)


def _rope_tables(seq):
    pos = jnp.arange(seq, dtype=F32)

    def cs(dim):
        inv = ROPE_THETA ** (-jnp.arange(0, dim, 2, dtype=F32) / dim)
        ang = pos[:, None] * inv[None, :]
        return jnp.cos(ang), jnp.sin(ang)

    cos_a, sin_a = cs(A_ROPE_DIM)
    cos2, sin2 = jnp.tile(cos_a, (1, 2)), jnp.tile(sin_a, (1, 2))
    zeros = lambda n: jnp.zeros((seq, n), F32)
    scale = (A_NOPE_DIM + A_ROPE_DIM) ** -0.5 * LOG2E
    pad = LANES - A_NOPE_DIM - A_ROPE_DIM
    cq = jnp.concatenate([jnp.full((seq, A_NOPE_DIM), scale, F32), cos2 * scale, zeros(pad)], axis=1)
    sq = jnp.concatenate([zeros(A_NOPE_DIM), sin2 * scale, zeros(pad)], axis=1)
    ck = jnp.concatenate([cos2, zeros(LANES - A_ROPE_DIM)], axis=1)
    sk = jnp.concatenate([sin2, zeros(LANES - A_ROPE_DIM)], axis=1)
    ta = jnp.concatenate([cq, sq, ck, sk], axis=1)
    cos_b, sin_b = cs(HEAD_DIM)
    zb = jnp.zeros_like(sin_b)
    c = jnp.tile(cos_b, (1, 4))
    s_hi = jnp.tile(jnp.concatenate([zb, sin_b], axis=1), (1, 2))
    s_lo = jnp.tile(jnp.concatenate([-sin_b, zb], axis=1), (1, 2))
    tb = jnp.concatenate([c, s_hi, s_lo], axis=1)
    tb = jnp.concatenate([tb, tb * LOG2E], axis=1)
    return ta, tb


def _rot_half_cols(w):
    half = w.shape[1] // 2
    return jnp.concatenate([-w[:, half:], w[:, :half]], axis=1)


def _prep_even(w_in, q_norm, w_q_up, kv_norm, w_kv_up, w_out):
    d = w_in.shape[0]
    c_q, c_kv, k_pe, qkv_b = jnp.split(w_in, [A_Q_RANK, COL_QB, COL_QB + A_ROPE_DIM], axis=1)
    q_b, k_b, v_b = jnp.split(qkv_b, 3, axis=1)
    padc = jnp.zeros((d, LANES - A_ROPE_DIM), F32)
    win = jnp.concatenate([c_q, c_kv, q_b * HEAD_DIM ** -0.5, k_b, v_b,
                           k_pe, padc, _rot_half_cols(k_pe), padc], axis=1).astype(BF)
    wq3 = w_q_up.reshape(A_Q_RANK, A_HEADS, A_NOPE_DIM + A_ROPE_DIM)
    nope, rope = wq3[..., :A_NOPE_DIM], wq3[..., A_NOPE_DIM:]
    zpad = jnp.zeros((A_Q_RANK, A_HEADS, LANES - A_NOPE_DIM - A_ROPE_DIM), F32)
    wq = jnp.concatenate([nope, rope, zpad], axis=-1).reshape(A_Q_RANK, A_PAD).astype(BF)
    rope_rot = jnp.concatenate([-rope[..., A_ROPE_DIM // 2:], rope[..., :A_ROPE_DIM // 2]], axis=-1)
    wqr = jnp.concatenate([jnp.zeros_like(nope), rope_rot, zpad], axis=-1).reshape(A_Q_RANK, A_PAD).astype(BF)
    wkv3 = w_kv_up.reshape(A_KV_RANK, A_HEADS, A_NOPE_DIM + A_V_DIM)
    k_nope, v = wkv3[..., :A_NOPE_DIM], wkv3[..., A_NOPE_DIM:]
    zk = jnp.zeros_like(k_nope)
    wk = jnp.concatenate([k_nope, zk], axis=-1).reshape(A_KV_RANK, A_PAD).astype(BF)
    v_even = jnp.concatenate([v, jnp.zeros_like(v)], axis=-1)
    v_odd = jnp.concatenate([jnp.zeros_like(v), v], axis=-1)
    odd = (np.arange(A_HEADS) % 2 == 1)[None, :, None]
    wv = jnp.where(odd, v_odd, v_even).reshape(A_KV_RANK, A_PAD).astype(BF)
    e = np.zeros((LANES, A_PAD), np.float32)
    for hd in range(A_HEADS):
        for j in range(A_ROPE_DIM):
            e[j, hd * LANES + A_NOPE_DIM + j] = 1.0
    wkv = jnp.concatenate([jnp.concatenate([wk, wv], axis=1),
                           jnp.concatenate([jnp.asarray(e, BF), jnp.zeros((LANES, A_PAD), BF)], axis=1)], axis=0)
    return dict(win=win, qg=q_norm[None, :], kvg=kv_norm[None, :], wq=wq, wqr=wqr, wkv=wkv, wo=w_out.astype(BF))


def kernel(x, ev_norm, ev_w_in, ev_q_norm, ev_w_q_up, ev_kv_norm, ev_w_kv_up, ev_w_out, od_norm, od_w_qkv, od_rpb, od_w_out, ffn_norm, ffn_w_gate, ffn_w_up, ffn_w_down, final_norm):
    batch, seq, d = x.shape
    h = x.reshape(batch * seq, d)
    ta, tb = _rope_tables(seq)
    for layer in range(DEPTH):
        i = layer // 2
        if layer % 2 == 0:
            w = _prep_even(ev_w_in[i], ev_q_norm[i], ev_w_q_up[i], ev_kv_norm[i], ev_w_kv_up[i], ev_w_out[i])
            qa, ka, va, qb, kb, vb, qg, kg, vg = _proj_even(h, ev_norm[i][None, :], w, ta, tb, seq)
            o_a = _mla(qa, ka, va, batch, seq)
            o_b = _dilated(qb, kb, vb, qg, kg, vg, batch, seq)
            attn, wo = [o_a, o_b], w["wo"]
        else:
            wqkv = jnp.concatenate([od_w_qkv[i][:, :C_WIDTH] * HEAD_DIM ** -0.5, od_w_qkv[i][:, C_WIDTH:]],
                                   axis=1).astype(BF)
            qkv = _proj_odd(h, od_norm[i][None, :], wqkv)
            attn = [_neighbourhood(qkv, _na_bias_slabs(od_rpb[i]), batch, seq)]
            wo = od_w_out[i].astype(BF)
        h = _post(h, attn, wo, ffn_norm[layer][None, :], ffn_w_gate[layer].astype(BF),
                  ffn_w_up[layer].astype(BF), ffn_w_down[layer].astype(BF),
                  final_g=final_norm[None, :] if layer == DEPTH - 1 else None)
    return h.reshape(batch, seq, d)
```

```python
import functools

import numpy as np
import jax
import jax.numpy as jnp
from jax import lax
from jax.experimental import pallas as pl
from jax.experimental.pallas import tpu as pltpu

BF = jnp.bfloat16
F32 = jnp.float32

D_MODEL = 1024
DEPTH = 4
HEAD_DIM = 64
A_HEADS = 8
A_Q_RANK = 256
A_KV_RANK = 128
A_NOPE_DIM = 64
A_ROPE_DIM = 32
A_V_DIM = 64
B_HEADS = 8
B_BRANCHES = ((128, 1), (512, 4), (2048, 16))
C_HEADS = 16
GRID_W = 64
NA_ROWS = 8
NA_COLS = 16
D_FF = -(-8 * D_MODEL // (3 * 256)) * 256
ROPE_THETA = 10000.0
EPS = 1e-6
NEG_INF = -1e30
LOG2E = float(np.log2(np.e))
LN2 = float(np.log(2.0))

LANES = 128
B_WIDTH = B_HEADS * HEAD_DIM
C_WIDTH = C_HEADS * HEAD_DIM
A_PAD = A_HEADS * LANES
IN_COLS = A_Q_RANK + A_KV_RANK + 3 * B_WIDTH + 2 * LANES
COL_QB = A_Q_RANK + A_KV_RANK
COL_KB = COL_QB + B_WIDTH
COL_VB = COL_KB + B_WIDTH
COL_KPE = COL_VB + B_WIDTH
COL_KPE_ROT = COL_KPE + LANES
HALF_WINDOW = 64
NA_QROWS = 2
NA_KROWS = 10
NA_VARIANTS = 5
NA_UNROLL = 16
RES_GROUPS = 16
DEINT = 4
DIL_QB = 128
DIL_KB = 256
DIL_UNROLL = 16
MLA_GROUP = 2
VMEM_LIMIT = 56 * 1024 * 1024


def _rms(x, g):
    return x * lax.rsqrt(jnp.mean(x * x, axis=-1, keepdims=True) + EPS) * g


def _dot(a, b):
    return jnp.dot(a, b, preferred_element_type=F32)


def _dot_nt(a, b):
    return lax.dot_general(a, b, (((1,), (1,)), ((), ())), preferred_element_type=F32)


def _loop(trips, body):
    if trips == 1:
        body(0, 0)
    else:
        lax.fori_loop(0, trips, body, 0)


def _clip(x, lo, hi):
    return min(max(x, lo), hi) if isinstance(x, int) else jnp.clip(x, lo, hi)


def _select(pred, a, b):
    return (a if pred else b) if isinstance(pred, bool) else jnp.where(pred, a, b)


def _aligned(x, m):
    return x if isinstance(x, int) else pl.multiple_of(x, m)


def _const_spec(shape):
    nd = len(shape)
    return pl.BlockSpec(shape, lambda *_: (0,) * nd, pipeline_mode=pl.Buffered(1))


def _weight_spec(w):
    if isinstance(w, tuple):
        stack, idx = w
        nd = stack.ndim - 1
        return pl.BlockSpec((None,) + stack.shape[1:], lambda *_: (idx,) + (0,) * nd, pipeline_mode=pl.Buffered(1))
    return _const_spec(w.shape)


def _weight_arrays(ws):
    return [w[0] if isinstance(w, tuple) else w for w in ws]


def _params(n_grid):
    return pltpu.CompilerParams(dimension_semantics=("arbitrary",) * n_grid,
                                vmem_limit_bytes=VMEM_LIMIT)


def _proj_even_kernel(h_ref, g_ref, win_ref, qg_ref, kvg_ref, wq_ref, wqr_ref, wkv_ref,
                      ta_ref, tb_ref, qa_ref, ka_ref, va_ref, qb_ref, kb_ref, vb_ref,
                      qr_ref, kr_ref, vr_ref, stage_ref, mid_ref):
    xn = _rms(h_ref[...], g_ref[...]).astype(BF)
    z = _dot(xn, win_ref[...])
    cq = _rms(z[:, 0:A_Q_RANK], qg_ref[...]).astype(BF)
    ckv = _rms(z[:, A_Q_RANK:COL_QB], kvg_ref[...]).astype(BF)
    qa = _dot(cq, wq_ref[...])
    qr = _dot(cq, wqr_ref[...])
    cq_t, sq_t = ta_ref[:, 0:LANES], ta_ref[:, LANES:2 * LANES]
    ck_t, sk_t = ta_ref[:, 2 * LANES:3 * LANES], ta_ref[:, 3 * LANES:4 * LANES]
    for hd in range(A_HEADS):
        sl = slice(hd * LANES, (hd + 1) * LANES)
        qa_ref[:, sl] = (qa[:, sl] * cq_t + qr[:, sl] * sq_t).astype(BF)
    kpe = z[:, COL_KPE:COL_KPE + LANES] * ck_t + z[:, COL_KPE_ROT:COL_KPE_ROT + LANES] * sk_t
    kv = _dot(jnp.concatenate([ckv, kpe.astype(BF)], axis=1), wkv_ref[...])
    ka_ref[...] = kv[:, 0:A_PAD].astype(BF)
    col = lax.broadcasted_iota(jnp.int32, (1, A_PAD), 1)
    own_half = ((col // HEAD_DIM) % 2) == ((col // LANES) % 2)
    va_ref[...] = (kv[:, A_PAD:] + jnp.where(own_half, 0.0, 1.0)).astype(BF)
    for p in range(B_WIDTH // LANES):
        for idx, (col, out) in enumerate(((COL_QB, qb_ref), (COL_KB, kb_ref))):
            t0 = (1 - idx) * 3 * LANES
            c_t, s_hi, s_lo = (tb_ref[:, t0 + n * LANES:t0 + (n + 1) * LANES] for n in range(3))
            xs = z[:, col + p * LANES: col + (p + 1) * LANES]
            r = (xs * c_t + pltpu.roll(xs, HEAD_DIM // 2, 1) * s_hi
                 + pltpu.roll(xs, LANES - HEAD_DIM // 2, 1) * s_lo)
            out[:, p * LANES:(p + 1) * LANES] = r.astype(BF)
            stage_ref[idx, p] = r
        stage_ref[2, p] = z[:, COL_VB + p * LANES:COL_VB + (p + 1) * LANES]
    vb_ref[...] = z[:, COL_VB:COL_KPE].astype(BF)
    tm = stage_ref.shape[2]
    for idx, out in enumerate((qr_ref, kr_ref, vr_ref)):
        for p in range(B_WIDTH // LANES):
            for b in range(DEINT):
                mid_ref[idx, p, b] = stage_ref[idx, p, pl.ds(b, tm // DEINT, stride=DEINT), :]
            for r in range(RES_GROUPS):
                out[r, :, p * LANES:(p + 1) * LANES] = (
                    mid_ref[idx, p, r % DEINT, pl.ds(r // DEINT, tm // RES_GROUPS, stride=DEINT), :].astype(BF))


def _proj_even(h, g, w, ta, tb, seq, tm=512):
    T = h.shape[0]
    nseq = seq // tm
    row = lambda c: pl.BlockSpec((tm, c), lambda i: (i, 0))
    tab = lambda c: pl.BlockSpec((tm, c), lambda i: (i % nseq, 0))
    consts = [g, w["win"], w["qg"], w["kvg"], w["wq"], w["wqr"], w["wkv"]]
    out_cols = (A_PAD, A_PAD, A_PAD, B_WIDTH, B_WIDTH, B_WIDTH)
    grouped = pl.BlockSpec((None, RES_GROUPS, tm // RES_GROUPS, B_WIDTH), lambda i: (i // nseq, 0, i % nseq, 0))
    grouped_shape = jax.ShapeDtypeStruct((T // seq, RES_GROUPS, seq // RES_GROUPS, B_WIDTH), BF)
    return pl.pallas_call(
        _proj_even_kernel,
        grid=(T // tm,),
        in_specs=[row(D_MODEL)] + [_const_spec(c.shape) for c in consts] + [tab(4 * LANES), tab(6 * LANES)],
        out_specs=[row(c) for c in out_cols] + [grouped] * 3,
        out_shape=[jax.ShapeDtypeStruct((T, c), BF) for c in out_cols] + [grouped_shape] * 3,
        scratch_shapes=[pltpu.VMEM((3, B_WIDTH // LANES, tm, LANES), F32),
                        pltpu.VMEM((3, B_WIDTH // LANES, DEINT, tm // DEINT, LANES), F32)],
        compiler_params=_params(1),
        name="proj_even",
    )(h, *consts, ta, tb)


def _mla_kernel(q_ref, k_ref, v_ref, o_ref):
    low = _low_half(q_ref.shape[0])
    tile = lambda hd: slice(hd * LANES, (hd + 1) * LANES)
    for g in range(A_HEADS // MLA_GROUP):
        heads = range(g * MLA_GROUP, (g + 1) * MLA_GROUP)
        scores = [_dot_nt(q_ref[:, tile(hd)], k_ref[:, tile(hd)]) for hd in heads]
        probs = [_exp_rows(s)[0] for s in scores]
        outs = [_dot(p, v_ref[:, tile(hd)]) for p, hd in zip(probs, heads)]
        for pair in range(MLA_GROUP // 2):
            o_ref[:, tile(g * MLA_GROUP // 2 + pair)] = _pair_finish(outs[2 * pair], outs[2 * pair + 1], low).astype(BF)


def _mla(qa, ka, va, batch, seq, tq=1024):
    q3, k3, v3 = (t.reshape(batch, seq, A_PAD) for t in (qa, ka, va))
    out = pl.pallas_call(
        _mla_kernel,
        grid=(batch, seq // tq),
        in_specs=[pl.BlockSpec((None, tq, A_PAD), lambda b, i: (b, i, 0)),
                  pl.BlockSpec((None, seq, A_PAD), lambda b, i: (b, 0, 0)),
                  pl.BlockSpec((None, seq, A_PAD), lambda b, i: (b, 0, 0))],
        out_specs=pl.BlockSpec((None, tq, B_WIDTH), lambda b, i: (b, i, 0)),
        out_shape=jax.ShapeDtypeStruct((batch, seq, A_HEADS * A_V_DIM), BF),
        compiler_params=_params(2),
        name="mla_attention",
    )(q3, k3, v3)
    return out.reshape(batch * seq, A_HEADS * A_V_DIM)


def _low_half(n):
    return lax.broadcasted_iota(jnp.int32, (n, LANES), 1) < HEAD_DIM


def _pair_scores(q2, k2, bias, low_q):
    zero = jnp.zeros_like(q2)
    return [_dot_nt(jnp.where(low_q, q2, zero), k2) + bias, _dot_nt(jnp.where(low_q, zero, q2), k2) + bias]


def _pair_values(v2, low_k):
    one = jnp.ones_like(v2)
    return [jnp.where(low_k, v2, one), jnp.where(low_k, one, v2)]


def _exp_rows(s):
    m = jnp.max(s, axis=-1, keepdims=True)
    return jnp.exp2(s - m).astype(BF), m


def _pair_finish(o0, o1, low_q, m0=None, m1=None):
    den = pltpu.roll(jnp.where(low_q, o1, o0), HEAD_DIM, 1)
    out = jnp.where(low_q, o0, o1) / den
    if m0 is None:
        return out
    return out, jnp.where(low_q, m0, m1) * LN2 + jnp.log(den)


def _attend_blocks(blocks, low_q, low_k):
    scores = [_pair_scores(q2, k2, bias, low_q) for q2, k2, _, bias in blocks]
    probs = [[_exp_rows(s) for s in pair] for pair in scores]
    results = []
    for (_, _, v2, _), pair in zip(blocks, probs):
        vals = _pair_values(v2, low_k)
        o0, o1 = _dot(pair[0][0], vals[0]), _dot(pair[1][0], vals[1])
        results.append(_pair_finish(o0, o1, low_q, pair[0][1], pair[1][1]))
    return results


def _grouped_branch(qg_ref, kg_ref, vg_ref, mask_ref, mask_base, o_ref, l_ref, dil):
    n16 = qg_ref.shape[1]
    per, jq, jk, n_jb = _grouped_geometry(dil, n16)
    nq, nk = per * jq, per * jk
    low_q, low_k = _low_half(nq), _low_half(nk)
    r_unroll = max(1, DIL_UNROLL // n_jb)

    def body(i, carry):
        blocks, where = [], []
        for ru in range(r_unroll):
            rr = i * r_unroll + ru
            for jb in range(n_jb):
                q0, k0 = _grouped_window(jb, per, jq, jk, n16)
                cat = lambda ref, lo, n: jnp.concatenate(
                    [ref[rr + dil * u, lo:lo + n, :] for u in range(per)], axis=0)
                blocks.append((cat(qg_ref, q0, jq), cat(kg_ref, k0, jk), cat(vg_ref, k0, jk),
                               mask_ref[mask_base + jb, :, 0:nk]))
                where.append((rr, q0))
        for (rr, q0), (o, l) in zip(where, _attend_blocks(blocks, low_q, low_k)):
            for u in range(per):
                o_ref[rr + dil * u, q0:q0 + jq, :] = o[u * jq:(u + 1) * jq]
                l_ref[rr + dil * u, q0:q0 + jq, :] = l[u * jq:(u + 1) * jq]
        return carry

    _loop(dil // r_unroll, body)


def _grouped_geometry(dil, n16):
    per = RES_GROUPS // dil
    jq = DIL_QB // per
    jk = min(DIL_KB, n16 * per) // per
    return per, jq, jk, n16 // jq


def _grouped_window(jb, per, jq, jk, n16):
    q0 = jq * jb
    return q0, int(np.clip(q0 - HALF_WINDOW // per, 0, n16 - jk))


def _dilated_masks(seq):
    n16 = seq // RES_GROUPS
    neg = np.float32(NEG_INF)
    q = np.arange(DIL_QB)[:, None]
    k = np.arange(DIL_KB)[None, :]
    tiles = []
    n_blocks = seq // DIL_QB
    for a in (0, 1, n_blocks - 1):
        qs = a * DIL_QB
        ks = int(np.clip(qs - HALF_WINDOW, 0, seq - DIL_KB))
        tiles.append(np.where(np.abs((k + ks) - (q + qs)) <= HALF_WINDOW, 0, neg))
    for _, dil in B_BRANCHES[1:]:
        per, jq, jk, n_jb = _grouped_geometry(dil, n16)
        pos = lambda i, n: per * (i % n) + i // n
        for jb in range(n_jb):
            q0, k0 = _grouped_window(jb, per, jq, jk, n16)
            rel = (pos(k, jk) + per * k0) - (pos(q, jq) + per * q0)
            tile = np.where(np.abs(rel) <= HALF_WINDOW, 0, neg)
            tile[:, per * jk:] = neg
            tiles.append(tile)
    return jnp.asarray(np.stack(tiles).astype(np.float32))


def _dilated_kernel(qn_ref, kn_ref, vn_ref, qg_ref, kg_ref, vg_ref, mask_ref, out_ref,
                    o1_ref, l1_ref, o2_ref, l2_ref, o3_ref, l3_ref, mix_ref, om_ref, lm_ref, mm_ref):
    seq = qn_ref.shape[0]
    n16 = seq // RES_GROUPS
    n_blocks = seq // DIL_QB
    low_q, low_k = _low_half(DIL_QB), _low_half(DIL_KB)

    def body(i, carry):
        blocks, starts = [], []
        for u in range(DIL_UNROLL):
            a = i * DIL_UNROLL + u
            qs = _aligned(a * DIL_QB, DIL_QB)
            ks = _aligned(_clip(a * DIL_QB - HALF_WINDOW, 0, seq - DIL_KB), HALF_WINDOW)
            variant = _select(a == 0, 0, _select(a == n_blocks - 1, 2, 1))
            blocks.append((qn_ref[pl.ds(qs, DIL_QB), :], kn_ref[pl.ds(ks, DIL_KB), :],
                           vn_ref[pl.ds(ks, DIL_KB), :], mask_ref[variant]))
            starts.append(qs)
        for qs, (o, l) in zip(starts, _attend_blocks(blocks, low_q, low_k)):
            o1_ref[pl.ds(qs, DIL_QB), :] = o
            l1_ref[pl.ds(qs, DIL_QB), :] = l
        return carry

    _loop(n_blocks // DIL_UNROLL, body)
    base4 = 3
    base16 = base4 + _grouped_geometry(B_BRANCHES[1][1], n16)[3]
    _grouped_branch(qg_ref, kg_ref, vg_ref, mask_ref, base4, o2_ref, l2_ref, B_BRANCHES[1][1])
    _grouped_branch(qg_ref, kg_ref, vg_ref, mask_ref, base16, o3_ref, l3_ref, B_BRANCHES[2][1])
    for b in range(DEINT):
        quarter = pl.ds(b, seq // DEINT, stride=DEINT)
        om_ref[b] = o1_ref[quarter, :]
        lm_ref[b] = l1_ref[quarter, :]
    for r in range(RES_GROUPS):
        rows = pl.ds(r // DEINT, n16, stride=DEINT)
        la, lb, lc = lm_ref[r % DEINT, rows, :], l2_ref[r], l3_ref[r]
        m = jnp.maximum(jnp.maximum(la, lb), lc)
        ea, eb, ec = jnp.exp(la - m), jnp.exp(lb - m), jnp.exp(lc - m)
        mm_ref[r % DEINT, rows, :] = (ea * om_ref[r % DEINT, rows, :] + eb * o2_ref[r] + ec * o3_ref[r]) / (ea + eb + ec)
    for b in range(DEINT):
        mix_ref[pl.ds(b, seq // DEINT, stride=DEINT), :] = mm_ref[b]
    out_ref[...] = mix_ref[...].astype(BF)


def _dilated(qn, kn, vn, qg, kg, vg, batch, seq):
    n16 = seq // RES_GROUPS
    nat = pl.BlockSpec((None, seq, LANES), lambda b, hp: (b, 0, hp))
    grp = pl.BlockSpec((None, RES_GROUPS, n16, LANES), lambda b, hp: (b, 0, 0, hp))
    nat3 = lambda t: t.reshape(batch, seq, B_WIDTH)
    masks = _dilated_masks(seq)
    out = pl.pallas_call(
        _dilated_kernel,
        grid=(batch, B_WIDTH // LANES),
        in_specs=[nat, nat, nat, grp, grp, grp, _const_spec(masks.shape)],
        out_specs=nat,
        out_shape=jax.ShapeDtypeStruct((batch, seq, B_WIDTH), BF),
        scratch_shapes=[pltpu.VMEM((seq, LANES), F32), pltpu.VMEM((seq, LANES), F32)]
        + [pltpu.VMEM((RES_GROUPS, n16, LANES), F32)] * 4 + [pltpu.VMEM((seq, LANES), F32)]
        + [pltpu.VMEM((DEINT, seq // DEINT, LANES), F32)] * 3,
        compiler_params=_params(2),
        name="dilated_attention",
    )(nat3(qn), nat3(kn), nat3(vn), qg, kg, vg, masks)
    return out.reshape(batch * seq, B_WIDTH)


def _proj_odd_kernel(h_ref, g_ref, w_ref, o_ref):
    xn = _rms(h_ref[...], g_ref[...]).astype(BF)
    z = _dot(xn, w_ref[...])
    o_ref[:, 0:C_WIDTH] = (z[:, 0:C_WIDTH] * LOG2E).astype(BF)
    o_ref[:, C_WIDTH:] = z[:, C_WIDTH:].astype(BF)


def _proj_odd(h, g, w, tm=512):
    T = h.shape[0]
    n = w[0].shape[-1]
    return pl.pallas_call(
        _proj_odd_kernel,
        grid=(T // tm,),
        in_specs=[pl.BlockSpec((tm, D_MODEL), lambda i: (i, 0)), _const_spec(g.shape), _weight_spec(w)],
        out_specs=pl.BlockSpec((tm, n), lambda i: (i, 0)),
        out_shape=jax.ShapeDtypeStruct((T, n), BF),
        compiler_params=_params(1),
        name="proj_odd",
    )(h, g, w[0])


def _na_kernel(q_ref, k_ref, v_ref, slab_ref, o_ref, vm_ref, b_ref, *, seq):
    rows = seq // GRID_W
    nq = NA_QROWS * GRID_W
    nk = NA_KROWS * GRID_W
    low = _low_half(nq)

    @pl.when(pl.program_id(1) == 0)
    def _():
        low_c = _low_half(GRID_W)
        index = _na_slab_index(seq)
        for v, qr, kp in np.ndindex(NA_VARIANTS, NA_QROWS, NA_KROWS // 2):
            i0, i1 = int(index[v, qr, 2 * kp]), int(index[v, qr, 2 * kp + 1])
            for j in range(2):
                b_ref[j, v, qr * GRID_W:(qr + 1) * GRID_W, kp * LANES:(kp + 1) * LANES] = jnp.where(
                    low_c, slab_ref[j, i0], slab_ref[j, i1])

    vals = _pair_values(v_ref[...], _low_half(seq))
    vm_ref[0] = vals[0]
    vm_ref[1] = vals[1]
    n_blocks = rows // NA_QROWS

    def body(i, carry):
        scores, where = [], []
        for u in range(NA_UNROLL):
            a = i * NA_UNROLL + u
            qs = _aligned(a * nq, nq)
            ks = _aligned(_clip(NA_QROWS * a - NA_ROWS // 2, 0, rows - NA_KROWS) * GRID_W, GRID_W)
            var = _select(a < 2, a, _select(a >= n_blocks - 2, a - (n_blocks - NA_VARIANTS), 2))
            q2 = q_ref[pl.ds(qs, nq), :]
            k2 = k_ref[pl.ds(ks, nk), :]
            zero = jnp.zeros_like(q2)
            scores.append([_dot_nt(jnp.where(low, q2, zero), k2) + b_ref[0, var],
                           _dot_nt(jnp.where(low, zero, q2), k2) + b_ref[1, var]])
            where.append((qs, ks))
        probs = [[_exp_rows(s)[0] for s in pair] for pair in scores]
        for (qs, ks), pair in zip(where, probs):
            o0 = _dot(pair[0], vm_ref[0, pl.ds(ks, nk), :])
            o1 = _dot(pair[1], vm_ref[1, pl.ds(ks, nk), :])
            o_ref[pl.ds(qs, nq), :] = _pair_finish(o0, o1, low).astype(BF)
        return carry

    _loop(n_blocks // NA_UNROLL, body)


def _na_slab_index(seq):
    rows = seq // GRID_W
    n_blocks = rows // NA_QROWS
    n_off = 2 * NA_ROWS - 1
    idx = np.full((NA_VARIANTS, NA_QROWS, NA_KROWS), n_off, np.int32)
    for v, a in enumerate((0, 1, 2, n_blocks - 2, n_blocks - 1)):
        ws = int(np.clip(NA_QROWS * a - NA_ROWS // 2, 0, rows - NA_KROWS))
        for qr in range(NA_QROWS):
            r = NA_QROWS * a + qr
            rs = int(np.clip(r - NA_ROWS // 2, 0, rows - NA_ROWS))
            for kr in range(NA_KROWS):
                krow = ws + kr
                if rs <= krow < rs + NA_ROWS:
                    idx[v, qr, kr] = krow - r + NA_ROWS - 1
    return idx


def _na_bias_slabs(rpb):
    col = np.arange(GRID_W)
    cs = np.clip(col - NA_COLS // 2, 0, GRID_W - NA_COLS)
    col_valid = (col[None, :] >= cs[:, None]) & (col[None, :] < cs[:, None] + NA_COLS)
    col_off = np.clip(col[None, :] - col[:, None] + NA_COLS - 1, 0, 2 * NA_COLS - 2)
    onehot = (col_off[..., None] == np.arange(2 * NA_COLS - 1)).astype(np.float32)
    toeplitz = jnp.einsum("hrd,ckd->hrck", rpb, onehot, precision=lax.Precision.HIGHEST)
    t2 = jnp.where(col_valid[None, None], toeplitz * LOG2E, NEG_INF)
    t2 = jnp.concatenate([t2, jnp.full_like(t2[:, :1], NEG_INF)], axis=1)
    return jnp.concatenate([t2, t2], axis=-1)


def _neighbourhood(qkv, slabs, batch, seq):
    qkv3 = qkv.reshape(batch, seq, 3 * C_WIDTH)
    n_hp = C_WIDTH // LANES
    blk = lambda off: pl.BlockSpec((None, seq, LANES), lambda hp, b: (b, 0, off + hp))
    out = pl.pallas_call(
        functools.partial(_na_kernel, seq=seq),
        grid=(n_hp, batch),
        in_specs=[blk(0), blk(n_hp), blk(2 * n_hp),
                  pl.BlockSpec((2,) + slabs.shape[1:], lambda hp, b: (hp, 0, 0, 0))],
        out_specs=pl.BlockSpec((None, seq, LANES), lambda hp, b: (b, 0, hp)),
        out_shape=jax.ShapeDtypeStruct((batch, seq, C_WIDTH), BF),
        scratch_shapes=[pltpu.VMEM((2, seq, LANES), BF),
                        pltpu.VMEM((2, NA_VARIANTS, NA_QROWS * GRID_W, NA_KROWS * GRID_W), F32)],
        compiler_params=_params(2),
        name="neighbourhood_attention",
    )(qkv3, qkv3, qkv3, slabs)
    return out.reshape(batch * seq, C_WIDTH)


def _post_kernel(*refs, n_attn, final):
    h_ref = refs[0]
    attn = refs[1:1 + n_attn]
    wo_ref, g_ref, wg_ref, wu_ref, wd_ref = refs[1 + n_attn:6 + n_attn]
    fg_ref = refs[6 + n_attn] if final else None
    out_ref = refs[-1]
    h1 = h_ref[...]
    off = 0
    for a in attn:
        width = a.shape[1]
        h1 = h1 + _dot(a[...], wo_ref[off:off + width, :])
        off += width
    xn = _rms(h1, g_ref[...]).astype(BF)
    gate = _dot(xn, wg_ref[...])
    up = _dot(xn, wu_ref[...])
    act = (gate * (1.0 / (1.0 + jnp.exp(-gate))) * up).astype(BF)
    y = h1 + _dot(act, wd_ref[...])
    if final:
        y = _rms(y, fg_ref[...])
    out_ref[...] = y


def _post(h, attn, wo, g, wg, wu, wd, final_g=None, tm=512):
    T = h.shape[0]
    final = final_g is not None
    row = lambda c: pl.BlockSpec((tm, c), lambda i: (i, 0))
    consts = [wo, g, wg, wu, wd] + ([final_g] if final else [])
    return pl.pallas_call(
        functools.partial(_post_kernel, n_attn=len(attn), final=final),
        grid=(T // tm,),
        in_specs=[row(D_MODEL)] + [row(a.shape[1]) for a in attn] + [_weight_spec(c) for c in consts],
        out_specs=row(D_MODEL),
        out_shape=jax.ShapeDtypeStruct((T, D_MODEL), F32),
        compiler_params=_params(1),
        name="outproj_ffn",
    )(h, *attn, *_weight_arrays(consts))


def _rope_tables(seq):
    pos = jnp.arange(seq, dtype=F32)

    def cs(dim):
        inv = ROPE_THETA ** (-jnp.arange(0, dim, 2, dtype=F32) / dim)
        ang = pos[:, None] * inv[None, :]
        return jnp.cos(ang), jnp.sin(ang)

    cos_a, sin_a = cs(A_ROPE_DIM)
    cos2, sin2 = jnp.tile(cos_a, (1, 2)), jnp.tile(sin_a, (1, 2))
    zeros = lambda n: jnp.zeros((seq, n), F32)
    scale = (A_NOPE_DIM + A_ROPE_DIM) ** -0.5 * LOG2E
    pad = LANES - A_NOPE_DIM - A_ROPE_DIM
    cq = jnp.concatenate([jnp.full((seq, A_NOPE_DIM), scale, F32), cos2 * scale, zeros(pad)], axis=1)
    sq = jnp.concatenate([zeros(A_NOPE_DIM), sin2 * scale, zeros(pad)], axis=1)
    ck = jnp.concatenate([cos2, zeros(LANES - A_ROPE_DIM)], axis=1)
    sk = jnp.concatenate([sin2, zeros(LANES - A_ROPE_DIM)], axis=1)
    ta = jnp.concatenate([cq, sq, ck, sk], axis=1)
    cos_b, sin_b = cs(HEAD_DIM)
    zb = jnp.zeros_like(sin_b)
    c = jnp.tile(cos_b, (1, 4))
    s_hi = jnp.tile(jnp.concatenate([zb, sin_b], axis=1), (1, 2))
    s_lo = jnp.tile(jnp.concatenate([-sin_b, zb], axis=1), (1, 2))
    tb = jnp.concatenate([c, s_hi, s_lo], axis=1)
    tb = jnp.concatenate([tb, tb * LOG2E], axis=1)
    return ta, tb


def _rot_half_cols(w):
    half = w.shape[1] // 2
    return jnp.concatenate([-w[:, half:], w[:, :half]], axis=1)


def _prep_even(w_in, q_norm, w_q_up, kv_norm, w_kv_up):
    d = w_in.shape[0]
    c_q, c_kv, k_pe, qkv_b = jnp.split(w_in, [A_Q_RANK, COL_QB, COL_QB + A_ROPE_DIM], axis=1)
    q_b, k_b, v_b = jnp.split(qkv_b, 3, axis=1)
    padc = jnp.zeros((d, LANES - A_ROPE_DIM), F32)
    win = jnp.concatenate([c_q, c_kv, q_b * HEAD_DIM ** -0.5, k_b, v_b,
                           k_pe, padc, _rot_half_cols(k_pe), padc], axis=1).astype(BF)
    wq3 = w_q_up.reshape(A_Q_RANK, A_HEADS, A_NOPE_DIM + A_ROPE_DIM)
    nope, rope = wq3[..., :A_NOPE_DIM], wq3[..., A_NOPE_DIM:]
    zpad = jnp.zeros((A_Q_RANK, A_HEADS, LANES - A_NOPE_DIM - A_ROPE_DIM), F32)
    wq = jnp.concatenate([nope, rope, zpad], axis=-1).reshape(A_Q_RANK, A_PAD).astype(BF)
    rope_rot = jnp.concatenate([-rope[..., A_ROPE_DIM // 2:], rope[..., :A_ROPE_DIM // 2]], axis=-1)
    wqr = jnp.concatenate([jnp.zeros_like(nope), rope_rot, zpad], axis=-1).reshape(A_Q_RANK, A_PAD).astype(BF)
    wkv3 = w_kv_up.reshape(A_KV_RANK, A_HEADS, A_NOPE_DIM + A_V_DIM)
    k_nope, v = wkv3[..., :A_NOPE_DIM], wkv3[..., A_NOPE_DIM:]
    zk = jnp.zeros_like(k_nope)
    wk = jnp.concatenate([k_nope, zk], axis=-1).reshape(A_KV_RANK, A_PAD).astype(BF)
    v_even = jnp.concatenate([v, jnp.zeros_like(v)], axis=-1)
    v_odd = jnp.concatenate([jnp.zeros_like(v), v], axis=-1)
    odd = (np.arange(A_HEADS) % 2 == 1)[None, :, None]
    wv = jnp.where(odd, v_odd, v_even).reshape(A_KV_RANK, A_PAD).astype(BF)
    e = np.zeros((LANES, A_PAD), np.float32)
    for hd in range(A_HEADS):
        for j in range(A_ROPE_DIM):
            e[j, hd * LANES + A_NOPE_DIM + j] = 1.0
    wkv = jnp.concatenate([jnp.concatenate([wk, wv], axis=1),
                           jnp.concatenate([jnp.asarray(e, BF), jnp.zeros((LANES, A_PAD), BF)], axis=1)], axis=0)
    return dict(win=win, qg=q_norm[None, :], kvg=kv_norm[None, :], wq=wq, wqr=wqr, wkv=wkv)


def kernel(x, ev_norm, ev_w_in, ev_q_norm, ev_w_q_up, ev_kv_norm, ev_w_kv_up, ev_w_out, od_norm, od_w_qkv, od_rpb, od_w_out, ffn_norm, ffn_w_gate, ffn_w_up, ffn_w_down, final_norm):
    batch, seq, d = x.shape
    h = x.reshape(batch * seq, d)
    ta, tb = _rope_tables(seq)
    wg_all, wu_all, wd_all = ffn_w_gate.astype(BF), ffn_w_up.astype(BF), ffn_w_down.astype(BF)
    ev_wo_all, od_wo_all = ev_w_out.astype(BF), od_w_out.astype(BF)
    wqkv_all = jnp.concatenate([od_w_qkv[:, :, :C_WIDTH] * HEAD_DIM ** -0.5, od_w_qkv[:, :, C_WIDTH:]],
                               axis=2).astype(BF)
    for layer in range(DEPTH):
        i = layer // 2
        if layer % 2 == 0:
            w = _prep_even(ev_w_in[i], ev_q_norm[i], ev_w_q_up[i], ev_kv_norm[i], ev_w_kv_up[i])
            qa, ka, va, qb, kb, vb, qg, kg, vg = _proj_even(h, ev_norm[i][None, :], w, ta, tb, seq)
            o_a = _mla(qa, ka, va, batch, seq)
            o_b = _dilated(qb, kb, vb, qg, kg, vg, batch, seq)
            attn, wo = [o_a, o_b], (ev_wo_all, i)
        else:
            qkv = _proj_odd(h, od_norm[i][None, :], (wqkv_all, i))
            attn = [_neighbourhood(qkv, _na_bias_slabs(od_rpb[i]), batch, seq)]
            wo = (od_wo_all, i)
        h = _post(h, attn, wo, ffn_norm[layer][None, :], (wg_all, layer), (wu_all, layer), (wd_all, layer),
                  final_g=final_norm[None, :] if layer == DEPTH - 1 else None)
    return h.reshape(batch, seq, d)
```

```python
import functools

import numpy as np
import jax
import jax.numpy as jnp
from jax import lax
from jax.experimental import pallas as pl
from jax.experimental.pallas import tpu as pltpu

BF = jnp.bfloat16
F32 = jnp.float32

D_MODEL = 1024
DEPTH = 4
HEAD_DIM = 64
A_HEADS = 8
A_Q_RANK = 256
A_KV_RANK = 128
A_NOPE_DIM = 64
A_ROPE_DIM = 32
A_V_DIM = 64
B_HEADS = 8
B_BRANCHES = ((128, 1), (512, 4), (2048, 16))
C_HEADS = 16
GRID_W = 64
NA_ROWS = 8
NA_COLS = 16
D_FF = -(-8 * D_MODEL // (3 * 256)) * 256
ROPE_THETA = 10000.0
EPS = 1e-6
NEG_INF = -1e30
LOG2E = float(np.log2(np.e))
LN2 = float(np.log(2.0))

LANES = 128
V7X_VMEM_BYTES = 64 * 1024 * 1024
B_WIDTH = B_HEADS * HEAD_DIM
C_WIDTH = C_HEADS * HEAD_DIM
A_PAD = A_HEADS * LANES
COL_QB = A_Q_RANK + A_KV_RANK
COL_KB = COL_QB + B_WIDTH
COL_VB = COL_KB + B_WIDTH
COL_KPE = COL_VB + B_WIDTH
COL_KPE_ROT = COL_KPE + LANES
HALF_WINDOW = B_BRANCHES[0][0] // (2 * B_BRANCHES[0][1])
assert all(w // (2 * d) == HALF_WINDOW for w, d in B_BRANCHES)
NA_QROWS = 2
NA_KROWS = NA_QROWS + NA_ROWS
NA_VARIANTS = 5
NA_UNROLL = 16
RES_GROUPS = B_BRANCHES[-1][1]
DEINT = 4
assert DEINT * DEINT == RES_GROUPS
DIL_QB = 2 * HALF_WINDOW
DIL_KB = DIL_QB + 2 * HALF_WINDOW
DIL_UNROLL = 16
DIL1_MASK_BLOCKS = (0, 1, -1)
MLA_GROUP = 2
VMEM_LIMIT = V7X_VMEM_BYTES - 8 * 1024 * 1024


def _rms(x, g):
    return x * lax.rsqrt(jnp.mean(x * x, axis=-1, keepdims=True) + EPS) * g


def _dot(a, b):
    return jnp.dot(a, b, preferred_element_type=F32)


def _dot_nt(a, b):
    return lax.dot_general(a, b, (((1,), (1,)), ((), ())), preferred_element_type=F32)


def _loop(trips, body):
    if trips == 1:
        body(0, 0)
    else:
        lax.fori_loop(0, trips, body, 0)


def _clip(x, lo, hi):
    return min(max(x, lo), hi) if isinstance(x, int) else jnp.clip(x, lo, hi)


def _select(pred, a, b):
    return (a if pred else b) if isinstance(pred, bool) else jnp.where(pred, a, b)


def _aligned(x, m):
    return x if isinstance(x, int) else pl.multiple_of(x, m)


def _const_spec(shape):
    nd = len(shape)
    return pl.BlockSpec(shape, lambda *_: (0,) * nd, pipeline_mode=pl.Buffered(1))


def _weight_spec(w):
    if isinstance(w, tuple):
        stack, idx = w
        nd = stack.ndim - 1
        return pl.BlockSpec((None,) + stack.shape[1:], lambda *_: (idx,) + (0,) * nd, pipeline_mode=pl.Buffered(1))
    return _const_spec(w.shape)


def _weight_arrays(ws):
    return [w[0] if isinstance(w, tuple) else w for w in ws]


def _params(n_grid):
    return pltpu.CompilerParams(dimension_semantics=("arbitrary",) * n_grid,
                                vmem_limit_bytes=VMEM_LIMIT)


def _proj_even_kernel(h_ref, g_ref, win_ref, qg_ref, kvg_ref, wq_ref, wqr_ref, wkv_ref,
                      ta_ref, tb_ref, qa_ref, ka_ref, va_ref, qb_ref, kb_ref, vb_ref,
                      qr_ref, kr_ref, vr_ref, stage_ref, mid_ref):
    xn = _rms(h_ref[...], g_ref[...]).astype(BF)
    z = _dot(xn, win_ref[...])
    cq = _rms(z[:, 0:A_Q_RANK], qg_ref[...]).astype(BF)
    ckv = _rms(z[:, A_Q_RANK:COL_QB], kvg_ref[...]).astype(BF)
    qa = _dot(cq, wq_ref[...])
    qr = _dot(cq, wqr_ref[...])
    cq_t, sq_t = ta_ref[:, 0:LANES], ta_ref[:, LANES:2 * LANES]
    ck_t, sk_t = ta_ref[:, 2 * LANES:3 * LANES], ta_ref[:, 3 * LANES:4 * LANES]
    for hd in range(A_HEADS):
        sl = slice(hd * LANES, (hd + 1) * LANES)
        qa_ref[:, sl] = (qa[:, sl] * cq_t + qr[:, sl] * sq_t).astype(BF)
    kpe = z[:, COL_KPE:COL_KPE + LANES] * ck_t + z[:, COL_KPE_ROT:COL_KPE_ROT + LANES] * sk_t
    kv = _dot(jnp.concatenate([ckv, kpe.astype(BF)], axis=1), wkv_ref[...])
    ka_ref[...] = kv[:, 0:A_PAD].astype(BF)
    col = lax.broadcasted_iota(jnp.int32, (1, A_PAD), 1)
    own_half = ((col // HEAD_DIM) % 2) == ((col // LANES) % 2)
    va_ref[...] = (kv[:, A_PAD:] + jnp.where(own_half, 0.0, 1.0)).astype(BF)
    for p in range(B_WIDTH // LANES):
        for idx, (col, out) in enumerate(((COL_QB, qb_ref), (COL_KB, kb_ref))):
            t0 = (1 - idx) * 3 * LANES
            c_t, s_hi, s_lo = (tb_ref[:, t0 + n * LANES:t0 + (n + 1) * LANES] for n in range(3))
            xs = z[:, col + p * LANES: col + (p + 1) * LANES]
            r = (xs * c_t + pltpu.roll(xs, HEAD_DIM // 2, 1) * s_hi
                 + pltpu.roll(xs, LANES - HEAD_DIM // 2, 1) * s_lo)
            out[:, p * LANES:(p + 1) * LANES] = r.astype(BF)
            stage_ref[idx, p] = r
        stage_ref[2, p] = z[:, COL_VB + p * LANES:COL_VB + (p + 1) * LANES]
    vb_ref[...] = z[:, COL_VB:COL_KPE].astype(BF)
    tm = stage_ref.shape[2]
    for idx, out in enumerate((qr_ref, kr_ref, vr_ref)):
        for p in range(B_WIDTH // LANES):
            for b in range(DEINT):
                mid_ref[idx, p, b] = stage_ref[idx, p, pl.ds(b, tm // DEINT, stride=DEINT), :]
            for r in range(RES_GROUPS):
                out[r, :, p * LANES:(p + 1) * LANES] = (
                    mid_ref[idx, p, r % DEINT, pl.ds(r // DEINT, tm // RES_GROUPS, stride=DEINT), :].astype(BF))


def _proj_even(h, g, w, ta, tb, seq, tm=512):
    T = h.shape[0]
    nseq = seq // tm
    row = lambda c: pl.BlockSpec((tm, c), lambda i: (i, 0))
    tab = lambda c: pl.BlockSpec((tm, c), lambda i: (i % nseq, 0))
    consts = [g, w["win"], w["qg"], w["kvg"], w["wq"], w["wqr"], w["wkv"]]
    out_cols = (A_PAD, A_PAD, A_PAD, B_WIDTH, B_WIDTH, B_WIDTH)
    grouped = pl.BlockSpec((None, RES_GROUPS, tm // RES_GROUPS, B_WIDTH), lambda i: (i // nseq, 0, i % nseq, 0))
    grouped_shape = jax.ShapeDtypeStruct((T // seq, RES_GROUPS, seq // RES_GROUPS, B_WIDTH), BF)
    return pl.pallas_call(
        _proj_even_kernel,
        grid=(T // tm,),
        in_specs=[row(D_MODEL)] + [_const_spec(c.shape) for c in consts] + [tab(4 * LANES), tab(6 * LANES)],
        out_specs=[row(c) for c in out_cols] + [grouped] * 3,
        out_shape=[jax.ShapeDtypeStruct((T, c), BF) for c in out_cols] + [grouped_shape] * 3,
        scratch_shapes=[pltpu.VMEM((3, B_WIDTH // LANES, tm, LANES), F32),
                        pltpu.VMEM((3, B_WIDTH // LANES, DEINT, tm // DEINT, LANES), F32)],
        compiler_params=_params(1),
        name="proj_even",
    )(h, *consts, ta, tb)


def _mla_kernel(q_ref, k_ref, v_ref, o_ref):
    low = _low_half(q_ref.shape[0])
    tile = lambda hd: slice(hd * LANES, (hd + 1) * LANES)
    for g in range(A_HEADS // MLA_GROUP):
        heads = range(g * MLA_GROUP, (g + 1) * MLA_GROUP)
        scores = [_dot_nt(q_ref[:, tile(hd)], k_ref[:, tile(hd)]) for hd in heads]
        probs = [_exp_rows(s)[0] for s in scores]
        outs = [_dot(p, v_ref[:, tile(hd)]) for p, hd in zip(probs, heads)]
        for pair in range(MLA_GROUP // 2):
            o_ref[:, tile(g * MLA_GROUP // 2 + pair)] = _pair_finish(outs[2 * pair], outs[2 * pair + 1], low).astype(BF)


def _mla(qa, ka, va, batch, seq, tq=1024):
    q3, k3, v3 = (t.reshape(batch, seq, A_PAD) for t in (qa, ka, va))
    out = pl.pallas_call(
        _mla_kernel,
        grid=(batch, seq // tq),
        in_specs=[pl.BlockSpec((None, tq, A_PAD), lambda b, i: (b, i, 0)),
                  pl.BlockSpec((None, seq, A_PAD), lambda b, i: (b, 0, 0)),
                  pl.BlockSpec((None, seq, A_PAD), lambda b, i: (b, 0, 0))],
        out_specs=pl.BlockSpec((None, tq, B_WIDTH), lambda b, i: (b, i, 0)),
        out_shape=jax.ShapeDtypeStruct((batch, seq, A_HEADS * A_V_DIM), BF),
        compiler_params=_params(2),
        name="mla_attention",
    )(q3, k3, v3)
    return out.reshape(batch * seq, A_HEADS * A_V_DIM)


def _low_half(n):
    return lax.broadcasted_iota(jnp.int32, (n, LANES), 1) < HEAD_DIM


def _pair_scores(q2, k2, bias, low_q):
    zero = jnp.zeros_like(q2)
    return [_dot_nt(jnp.where(low_q, q2, zero), k2) + bias, _dot_nt(jnp.where(low_q, zero, q2), k2) + bias]


def _pair_values(v2, low_k):
    one = jnp.ones_like(v2)
    return [jnp.where(low_k, v2, one), jnp.where(low_k, one, v2)]


def _exp_rows(s):
    m = jnp.max(s, axis=-1, keepdims=True)
    return jnp.exp2(s - m).astype(BF), m


def _pair_finish(o0, o1, low_q, m0=None, m1=None):
    den = pltpu.roll(jnp.where(low_q, o1, o0), HEAD_DIM, 1)
    out = jnp.where(low_q, o0, o1) / den
    if m0 is None:
        return out
    return out, jnp.where(low_q, m0, m1) * LN2 + jnp.log(den)


def _attend_blocks(blocks, low_q, low_k):
    scores = [_pair_scores(q2, k2, bias, low_q) for q2, k2, _, bias in blocks]
    probs = [[_exp_rows(s) for s in pair] for pair in scores]
    results = []
    for (_, _, v2, _), pair in zip(blocks, probs):
        vals = _pair_values(v2, low_k)
        o0, o1 = _dot(pair[0][0], vals[0]), _dot(pair[1][0], vals[1])
        results.append(_pair_finish(o0, o1, low_q, pair[0][1], pair[1][1]))
    return results


def _grouped_branch(qg_ref, kg_ref, vg_ref, mask_ref, mask_base, o_ref, l_ref, dil):
    n16 = qg_ref.shape[1]
    per, jq, jk, n_jb = _grouped_geometry(dil, n16)
    nq, nk = per * jq, per * jk
    low_q, low_k = _low_half(nq), _low_half(nk)
    r_unroll = max(1, DIL_UNROLL // n_jb)

    def body(i, carry):
        blocks, where = [], []
        for ru in range(r_unroll):
            rr = i * r_unroll + ru
            for jb in range(n_jb):
                q0, k0 = _grouped_window(jb, per, jq, jk, n16)
                cat = lambda ref, lo, n: jnp.concatenate(
                    [ref[rr + dil * u, lo:lo + n, :] for u in range(per)], axis=0)
                blocks.append((cat(qg_ref, q0, jq), cat(kg_ref, k0, jk), cat(vg_ref, k0, jk),
                               mask_ref[mask_base + jb, :, 0:nk]))
                where.append((rr, q0))
        for (rr, q0), (o, l) in zip(where, _attend_blocks(blocks, low_q, low_k)):
            for u in range(per):
                o_ref[rr + dil * u, q0:q0 + jq, :] = o[u * jq:(u + 1) * jq]
                l_ref[rr + dil * u, q0:q0 + jq, :] = l[u * jq:(u + 1) * jq]
        return carry

    _loop(dil // r_unroll, body)


def _grouped_geometry(dil, n16):
    per = RES_GROUPS // dil
    jq = DIL_QB // per
    jk = min(DIL_KB, n16 * per) // per
    return per, jq, jk, n16 // jq


def _grouped_window(jb, per, jq, jk, n16):
    q0 = jq * jb
    return q0, int(np.clip(q0 - HALF_WINDOW // per, 0, n16 - jk))


def _dilated_masks(seq):
    n16 = seq // RES_GROUPS
    neg = np.float32(NEG_INF)
    q = np.arange(DIL_QB)[:, None]
    k = np.arange(DIL_KB)[None, :]
    tiles = []
    n_blocks = seq // DIL_QB
    for a in DIL1_MASK_BLOCKS:
        a %= n_blocks
        qs = a * DIL_QB
        ks = int(np.clip(qs - HALF_WINDOW, 0, seq - DIL_KB))
        tiles.append(np.where(np.abs((k + ks) - (q + qs)) <= HALF_WINDOW, 0, neg))
    for _, dil in B_BRANCHES[1:]:
        per, jq, jk, n_jb = _grouped_geometry(dil, n16)
        pos = lambda i, n: per * (i % n) + i // n
        for jb in range(n_jb):
            q0, k0 = _grouped_window(jb, per, jq, jk, n16)
            rel = (pos(k, jk) + per * k0) - (pos(q, jq) + per * q0)
            tile = np.where(np.abs(rel) <= HALF_WINDOW, 0, neg)
            tile[:, per * jk:] = neg
            tiles.append(tile)
    return jnp.asarray(np.stack(tiles).astype(np.float32))


def _dilated_kernel(qn_ref, kn_ref, vn_ref, qg_ref, kg_ref, vg_ref, mask_ref, out_ref,
                    o1_ref, l1_ref, o2_ref, l2_ref, o3_ref, l3_ref, mix_ref, om_ref, lm_ref, mm_ref):
    seq = qn_ref.shape[0]
    n16 = seq // RES_GROUPS
    n_blocks = seq // DIL_QB
    low_q, low_k = _low_half(DIL_QB), _low_half(DIL_KB)

    def body(i, carry):
        blocks, starts = [], []
        for u in range(DIL_UNROLL):
            a = i * DIL_UNROLL + u
            qs = _aligned(a * DIL_QB, DIL_QB)
            ks = _aligned(_clip(a * DIL_QB - HALF_WINDOW, 0, seq - DIL_KB), HALF_WINDOW)
            variant = _select(a == 0, 0, _select(a == n_blocks - 1, 2, 1))
            blocks.append((qn_ref[pl.ds(qs, DIL_QB), :], kn_ref[pl.ds(ks, DIL_KB), :],
                           vn_ref[pl.ds(ks, DIL_KB), :], mask_ref[variant]))
            starts.append(qs)
        for qs, (o, l) in zip(starts, _attend_blocks(blocks, low_q, low_k)):
            o1_ref[pl.ds(qs, DIL_QB), :] = o
            l1_ref[pl.ds(qs, DIL_QB), :] = l
        return carry

    _loop(n_blocks // DIL_UNROLL, body)
    base4 = len(DIL1_MASK_BLOCKS)
    base16 = base4 + _grouped_geometry(B_BRANCHES[1][1], n16)[3]
    _grouped_branch(qg_ref, kg_ref, vg_ref, mask_ref, base4, o2_ref, l2_ref, B_BRANCHES[1][1])
    _grouped_branch(qg_ref, kg_ref, vg_ref, mask_ref, base16, o3_ref, l3_ref, B_BRANCHES[2][1])
    for b in range(DEINT):
        quarter = pl.ds(b, seq // DEINT, stride=DEINT)
        om_ref[b] = o1_ref[quarter, :]
        lm_ref[b] = l1_ref[quarter, :]
    for r in range(RES_GROUPS):
        rows = pl.ds(r // DEINT, n16, stride=DEINT)
        la, lb, lc = lm_ref[r % DEINT, rows, :], l2_ref[r], l3_ref[r]
        m = jnp.maximum(jnp.maximum(la, lb), lc)
        ea, eb, ec = jnp.exp(la - m), jnp.exp(lb - m), jnp.exp(lc - m)
        mm_ref[r % DEINT, rows, :] = (ea * om_ref[r % DEINT, rows, :] + eb * o2_ref[r] + ec * o3_ref[r]) / (ea + eb + ec)
    for b in range(DEINT):
        mix_ref[pl.ds(b, seq // DEINT, stride=DEINT), :] = mm_ref[b]
    out_ref[...] = mix_ref[...].astype(BF)


def _dilated(qn, kn, vn, qg, kg, vg, batch, seq):
    n16 = seq // RES_GROUPS
    nat = pl.BlockSpec((None, seq, LANES), lambda b, hp: (b, 0, hp))
    grp = pl.BlockSpec((None, RES_GROUPS, n16, LANES), lambda b, hp: (b, 0, 0, hp))
    nat3 = lambda t: t.reshape(batch, seq, B_WIDTH)
    masks = _dilated_masks(seq)
    out = pl.pallas_call(
        _dilated_kernel,
        grid=(batch, B_WIDTH // LANES),
        in_specs=[nat, nat, nat, grp, grp, grp, _const_spec(masks.shape)],
        out_specs=nat,
        out_shape=jax.ShapeDtypeStruct((batch, seq, B_WIDTH), BF),
        scratch_shapes=[pltpu.VMEM((seq, LANES), F32), pltpu.VMEM((seq, LANES), F32)]
        + [pltpu.VMEM((RES_GROUPS, n16, LANES), F32)] * 4 + [pltpu.VMEM((seq, LANES), F32)]
        + [pltpu.VMEM((DEINT, seq // DEINT, LANES), F32)] * 3,
        compiler_params=_params(2),
        name="dilated_attention",
    )(nat3(qn), nat3(kn), nat3(vn), qg, kg, vg, masks)
    return out.reshape(batch * seq, B_WIDTH)


def _proj_odd_kernel(h_ref, g_ref, w_ref, o_ref):
    xn = _rms(h_ref[...], g_ref[...]).astype(BF)
    z = _dot(xn, w_ref[...])
    o_ref[:, 0:C_WIDTH] = (z[:, 0:C_WIDTH] * LOG2E).astype(BF)
    o_ref[:, C_WIDTH:] = z[:, C_WIDTH:].astype(BF)


def _proj_odd(h, g, w, tm=512):
    T = h.shape[0]
    n = w[0].shape[-1]
    return pl.pallas_call(
        _proj_odd_kernel,
        grid=(T // tm,),
        in_specs=[pl.BlockSpec((tm, D_MODEL), lambda i: (i, 0)), _const_spec(g.shape), _weight_spec(w)],
        out_specs=pl.BlockSpec((tm, n), lambda i: (i, 0)),
        out_shape=jax.ShapeDtypeStruct((T, n), BF),
        compiler_params=_params(1),
        name="proj_odd",
    )(h, g, w[0])


def _na_kernel(q_ref, k_ref, v_ref, slab_ref, o_ref, vm_ref, b_ref, *, seq):
    rows = seq // GRID_W
    nq = NA_QROWS * GRID_W
    nk = NA_KROWS * GRID_W
    low = _low_half(nq)

    @pl.when(pl.program_id(1) == 0)
    def _():
        low_c = _low_half(GRID_W)
        index = _na_slab_index(seq)
        for v, qr, kp in np.ndindex(NA_VARIANTS, NA_QROWS, NA_KROWS // 2):
            i0, i1 = int(index[v, qr, 2 * kp]), int(index[v, qr, 2 * kp + 1])
            for j in range(2):
                b_ref[j, v, qr * GRID_W:(qr + 1) * GRID_W, kp * LANES:(kp + 1) * LANES] = jnp.where(
                    low_c, slab_ref[j, i0], slab_ref[j, i1])

    vals = _pair_values(v_ref[...], _low_half(seq))
    vm_ref[0] = vals[0]
    vm_ref[1] = vals[1]
    n_blocks = rows // NA_QROWS

    def body(i, carry):
        scores, where = [], []
        for u in range(NA_UNROLL):
            a = i * NA_UNROLL + u
            qs = _aligned(a * nq, nq)
            ks = _aligned(_clip(NA_QROWS * a - NA_ROWS // 2, 0, rows - NA_KROWS) * GRID_W, GRID_W)
            var = _select(a < 2, a, _select(a >= n_blocks - 2, a - (n_blocks - NA_VARIANTS), 2))
            q2 = q_ref[pl.ds(qs, nq), :]
            k2 = k_ref[pl.ds(ks, nk), :]
            zero = jnp.zeros_like(q2)
            scores.append([_dot_nt(jnp.where(low, q2, zero), k2) + b_ref[0, var],
                           _dot_nt(jnp.where(low, zero, q2), k2) + b_ref[1, var]])
            where.append((qs, ks))
        probs = [[_exp_rows(s)[0] for s in pair] for pair in scores]
        for (qs, ks), pair in zip(where, probs):
            o0 = _dot(pair[0], vm_ref[0, pl.ds(ks, nk), :])
            o1 = _dot(pair[1], vm_ref[1, pl.ds(ks, nk), :])
            o_ref[pl.ds(qs, nq), :] = _pair_finish(o0, o1, low).astype(BF)
        return carry

    _loop(n_blocks // NA_UNROLL, body)


def _na_slab_index(seq):
    rows = seq // GRID_W
    n_blocks = rows // NA_QROWS
    n_off = 2 * NA_ROWS - 1
    idx = np.full((NA_VARIANTS, NA_QROWS, NA_KROWS), n_off, np.int32)
    for v, a in enumerate((0, 1, 2, n_blocks - 2, n_blocks - 1)):
        ws = int(np.clip(NA_QROWS * a - NA_ROWS // 2, 0, rows - NA_KROWS))
        for qr in range(NA_QROWS):
            r = NA_QROWS * a + qr
            rs = int(np.clip(r - NA_ROWS // 2, 0, rows - NA_ROWS))
            for kr in range(NA_KROWS):
                krow = ws + kr
                if rs <= krow < rs + NA_ROWS:
                    idx[v, qr, kr] = krow - r + NA_ROWS - 1
    return idx


def _na_bias_slabs(rpb):
    col = np.arange(GRID_W)
    cs = np.clip(col - NA_COLS // 2, 0, GRID_W - NA_COLS)
    col_valid = (col[None, :] >= cs[:, None]) & (col[None, :] < cs[:, None] + NA_COLS)
    col_off = np.clip(col[None, :] - col[:, None] + NA_COLS - 1, 0, 2 * NA_COLS - 2)
    onehot = (col_off[..., None] == np.arange(2 * NA_COLS - 1)).astype(np.float32)
    toeplitz = jnp.einsum("hrd,ckd->hrck", rpb, onehot, precision=lax.Precision.HIGHEST)
    t2 = jnp.where(col_valid[None, None], toeplitz * LOG2E, NEG_INF)
    t2 = jnp.concatenate([t2, jnp.full_like(t2[:, :1], NEG_INF)], axis=1)
    return jnp.concatenate([t2, t2], axis=-1)


def _neighbourhood(qkv, slabs, batch, seq):
    qkv3 = qkv.reshape(batch, seq, 3 * C_WIDTH)
    n_hp = C_WIDTH // LANES
    blk = lambda off: pl.BlockSpec((None, seq, LANES), lambda hp, b: (b, 0, off + hp))
    out = pl.pallas_call(
        functools.partial(_na_kernel, seq=seq),
        grid=(n_hp, batch),
        in_specs=[blk(0), blk(n_hp), blk(2 * n_hp),
                  pl.BlockSpec((2,) + slabs.shape[1:], lambda hp, b: (hp, 0, 0, 0))],
        out_specs=pl.BlockSpec((None, seq, LANES), lambda hp, b: (b, 0, hp)),
        out_shape=jax.ShapeDtypeStruct((batch, seq, C_WIDTH), BF),
        scratch_shapes=[pltpu.VMEM((2, seq, LANES), BF),
                        pltpu.VMEM((2, NA_VARIANTS, NA_QROWS * GRID_W, NA_KROWS * GRID_W), F32)],
        compiler_params=_params(2),
        name="neighbourhood_attention",
    )(qkv3, qkv3, qkv3, slabs)
    return out.reshape(batch * seq, C_WIDTH)


def _post_kernel(*refs, n_attn, final):
    h_ref = refs[0]
    attn = refs[1:1 + n_attn]
    wo_ref, g_ref, wg_ref, wu_ref, wd_ref = refs[1 + n_attn:6 + n_attn]
    fg_ref = refs[6 + n_attn] if final else None
    out_ref = refs[-1]
    h1 = h_ref[...]
    off = 0
    for a in attn:
        width = a.shape[1]
        h1 = h1 + _dot(a[...], wo_ref[off:off + width, :])
        off += width
    xn = _rms(h1, g_ref[...]).astype(BF)
    gate = _dot(xn, wg_ref[...])
    up = _dot(xn, wu_ref[...])
    act = (gate * (1.0 / (1.0 + jnp.exp(-gate))) * up).astype(BF)
    y = h1 + _dot(act, wd_ref[...])
    if final:
        y = _rms(y, fg_ref[...])
    out_ref[...] = y


def _post(h, attn, wo, g, wg, wu, wd, final_g=None, tm=512):
    T = h.shape[0]
    final = final_g is not None
    row = lambda c: pl.BlockSpec((tm, c), lambda i: (i, 0))
    consts = [wo, g, wg, wu, wd] + ([final_g] if final else [])
    return pl.pallas_call(
        functools.partial(_post_kernel, n_attn=len(attn), final=final),
        grid=(T // tm,),
        in_specs=[row(D_MODEL)] + [row(a.shape[1]) for a in attn] + [_weight_spec(c) for c in consts],
        out_specs=row(D_MODEL),
        out_shape=jax.ShapeDtypeStruct((T, D_MODEL), F32),
        compiler_params=_params(1),
        name="outproj_ffn",
    )(h, *attn, *_weight_arrays(consts))


def _rope_tables(seq):
    pos = jnp.arange(seq, dtype=F32)

    def cs(dim):
        inv = ROPE_THETA ** (-jnp.arange(0, dim, 2, dtype=F32) / dim)
        ang = pos[:, None] * inv[None, :]
        return jnp.cos(ang), jnp.sin(ang)

    cos_a, sin_a = cs(A_ROPE_DIM)
    cos2, sin2 = jnp.tile(cos_a, (1, 2)), jnp.tile(sin_a, (1, 2))
    zeros = lambda n: jnp.zeros((seq, n), F32)
    scale = (A_NOPE_DIM + A_ROPE_DIM) ** -0.5 * LOG2E
    pad = LANES - A_NOPE_DIM - A_ROPE_DIM
    cq = jnp.concatenate([jnp.full((seq, A_NOPE_DIM), scale, F32), cos2 * scale, zeros(pad)], axis=1)
    sq = jnp.concatenate([zeros(A_NOPE_DIM), sin2 * scale, zeros(pad)], axis=1)
    ck = jnp.concatenate([cos2, zeros(LANES - A_ROPE_DIM)], axis=1)
    sk = jnp.concatenate([sin2, zeros(LANES - A_ROPE_DIM)], axis=1)
    ta = jnp.concatenate([cq, sq, ck, sk], axis=1)
    cos_b, sin_b = cs(HEAD_DIM)
    zb = jnp.zeros_like(sin_b)
    c = jnp.tile(cos_b, (1, 4))
    s_hi = jnp.tile(jnp.concatenate([zb, sin_b], axis=1), (1, 2))
    s_lo = jnp.tile(jnp.concatenate([-sin_b, zb], axis=1), (1, 2))
    tb = jnp.concatenate([c, s_hi, s_lo], axis=1)
    tb = jnp.concatenate([tb, tb * LOG2E], axis=1)
    return ta, tb


def _rot_half_cols(w):
    half = w.shape[1] // 2
    return jnp.concatenate([-w[:, half:], w[:, :half]], axis=1)


def _prep_even(w_in, q_norm, w_q_up, kv_norm, w_kv_up):
    d = w_in.shape[0]
    c_q, c_kv, k_pe, qkv_b = jnp.split(w_in, [A_Q_RANK, COL_QB, COL_QB + A_ROPE_DIM], axis=1)
    q_b, k_b, v_b = jnp.split(qkv_b, 3, axis=1)
    padc = jnp.zeros((d, LANES - A_ROPE_DIM), F32)
    win = jnp.concatenate([c_q, c_kv, q_b * HEAD_DIM ** -0.5, k_b, v_b,
                           k_pe, padc, _rot_half_cols(k_pe), padc], axis=1).astype(BF)
    wq3 = w_q_up.reshape(A_Q_RANK, A_HEADS, A_NOPE_DIM + A_ROPE_DIM)
    nope, rope = wq3[..., :A_NOPE_DIM], wq3[..., A_NOPE_DIM:]
    zpad = jnp.zeros((A_Q_RANK, A_HEADS, LANES - A_NOPE_DIM - A_ROPE_DIM), F32)
    wq = jnp.concatenate([nope, rope, zpad], axis=-1).reshape(A_Q_RANK, A_PAD).astype(BF)
    rope_rot = jnp.concatenate([-rope[..., A_ROPE_DIM // 2:], rope[..., :A_ROPE_DIM // 2]], axis=-1)
    wqr = jnp.concatenate([jnp.zeros_like(nope), rope_rot, zpad], axis=-1).reshape(A_Q_RANK, A_PAD).astype(BF)
    wkv3 = w_kv_up.reshape(A_KV_RANK, A_HEADS, A_NOPE_DIM + A_V_DIM)
    k_nope, v = wkv3[..., :A_NOPE_DIM], wkv3[..., A_NOPE_DIM:]
    zk = jnp.zeros_like(k_nope)
    wk = jnp.concatenate([k_nope, zk], axis=-1).reshape(A_KV_RANK, A_PAD).astype(BF)
    v_even = jnp.concatenate([v, jnp.zeros_like(v)], axis=-1)
    v_odd = jnp.concatenate([jnp.zeros_like(v), v], axis=-1)
    odd = (np.arange(A_HEADS) % 2 == 1)[None, :, None]
    wv = jnp.where(odd, v_odd, v_even).reshape(A_KV_RANK, A_PAD).astype(BF)
    e = np.zeros((LANES, A_PAD), np.float32)
    for hd in range(A_HEADS):
        for j in range(A_ROPE_DIM):
            e[j, hd * LANES + A_NOPE_DIM + j] = 1.0
    wkv = jnp.concatenate([jnp.concatenate([wk, wv], axis=1),
                           jnp.concatenate([jnp.asarray(e, BF), jnp.zeros((LANES, A_PAD), BF)], axis=1)], axis=0)
    return dict(win=win, qg=q_norm[None, :], kvg=kv_norm[None, :], wq=wq, wqr=wqr, wkv=wkv)


def kernel(x, ev_norm, ev_w_in, ev_q_norm, ev_w_q_up, ev_kv_norm, ev_w_kv_up, ev_w_out, od_norm, od_w_qkv, od_rpb, od_w_out, ffn_norm, ffn_w_gate, ffn_w_up, ffn_w_down, final_norm):
    batch, seq, d = x.shape
    assert d == D_MODEL and seq % (DIL_QB * DIL_UNROLL) == 0 and (seq // RES_GROUPS) % DIL_QB == 0
    assert seq % (NA_QROWS * NA_UNROLL * GRID_W) == 0
    h = x.reshape(batch * seq, d)
    ta, tb = _rope_tables(seq)
    wg_all, wu_all, wd_all = ffn_w_gate.astype(BF), ffn_w_up.astype(BF), ffn_w_down.astype(BF)
    ev_wo_all, od_wo_all = ev_w_out.astype(BF), od_w_out.astype(BF)
    wqkv_all = jnp.concatenate([od_w_qkv[:, :, :C_WIDTH] * HEAD_DIM ** -0.5, od_w_qkv[:, :, C_WIDTH:]],
                               axis=2).astype(BF)
    for layer in range(DEPTH):
        i = layer // 2
        if layer % 2 == 0:
            w = _prep_even(ev_w_in[i], ev_q_norm[i], ev_w_q_up[i], ev_kv_norm[i], ev_w_kv_up[i])
            qa, ka, va, qb, kb, vb, qg, kg, vg = _proj_even(h, ev_norm[i][None, :], w, ta, tb, seq)
            o_a = _mla(qa, ka, va, batch, seq)
            o_b = _dilated(qb, kb, vb, qg, kg, vg, batch, seq)
            attn, wo = [o_a, o_b], (ev_wo_all, i)
        else:
            qkv = _proj_odd(h, od_norm[i][None, :], (wqkv_all, i))
            attn = [_neighbourhood(qkv, _na_bias_slabs(od_rpb[i]), batch, seq)]
            wo = (od_wo_all, i)
        h = _post(h, attn, wo, ffn_norm[layer][None, :], (wg_all, layer), (wu_all, layer), (wd_all, layer),
                  final_g=final_norm[None, :] if layer == DEPTH - 1 else None)
    return h.reshape(batch, seq, d)
```

```python
import functools

import numpy as np
import jax
import jax.numpy as jnp
from jax import lax
from jax.experimental import pallas as pl
from jax.experimental.pallas import tpu as pltpu

BF = jnp.bfloat16
F32 = jnp.float32

D_MODEL = 1024
DEPTH = 4
HEAD_DIM = 64
A_HEADS = 8
A_Q_RANK = 256
A_KV_RANK = 128
A_NOPE_DIM = 64
A_ROPE_DIM = 32
A_V_DIM = 64
B_HEADS = 8
B_BRANCHES = ((128, 1), (512, 4), (2048, 16))
C_HEADS = 16
GRID_W = 64
NA_ROWS = 8
NA_COLS = 16
D_FF = -(-8 * D_MODEL // (3 * 256)) * 256
ROPE_THETA = 10000.0
EPS = 1e-6
NEG_INF = -1e30
LOG2E = float(np.log2(np.e))
LN2 = float(np.log(2.0))

LANES = 128
V7X_VMEM_BYTES = 64 * 1024 * 1024
B_WIDTH = B_HEADS * HEAD_DIM
C_WIDTH = C_HEADS * HEAD_DIM
A_PAD = A_HEADS * LANES
COL_QB = A_Q_RANK + A_KV_RANK
COL_KB = COL_QB + B_WIDTH
COL_VB = COL_KB + B_WIDTH
COL_KPE = COL_VB + B_WIDTH
COL_KPE_ROT = COL_KPE + LANES
HALF_WINDOW = B_BRANCHES[0][0] // (2 * B_BRANCHES[0][1])
assert all(w // (2 * d) == HALF_WINDOW for w, d in B_BRANCHES)
NA_QROWS = 2
NA_KROWS = NA_QROWS + NA_ROWS
NA_VARIANTS = 5
NA_UNROLL = 16
RES_GROUPS = B_BRANCHES[-1][1]
DEINT = 4
assert DEINT * DEINT == RES_GROUPS
DIL_QB = 2 * HALF_WINDOW
DIL_KB = DIL_QB + 2 * HALF_WINDOW
DIL_UNROLL = 16
DIL1_MASK_BLOCKS = (0, 1, -1)
MLA_GROUP = 2
VMEM_LIMIT = V7X_VMEM_BYTES - 8 * 1024 * 1024


def _rms(x, g):
    return x * lax.rsqrt(jnp.mean(x * x, axis=-1, keepdims=True) + EPS) * g


def _dot(a, b):
    return jnp.dot(a, b, preferred_element_type=F32)


def _dot_nt(a, b):
    return lax.dot_general(a, b, (((1,), (1,)), ((), ())), preferred_element_type=F32)


def _loop(trips, body):
    if trips == 1:
        body(0, 0)
    else:
        lax.fori_loop(0, trips, body, 0)


def _clip(x, lo, hi):
    return min(max(x, lo), hi) if isinstance(x, int) else jnp.clip(x, lo, hi)


def _select(pred, a, b):
    return (a if pred else b) if isinstance(pred, bool) else jnp.where(pred, a, b)


def _aligned(x, m):
    return x if isinstance(x, int) else pl.multiple_of(x, m)


def _const_spec(shape):
    nd = len(shape)
    return pl.BlockSpec(shape, lambda *_: (0,) * nd, pipeline_mode=pl.Buffered(1))


def _weight_spec(w):
    if isinstance(w, tuple):
        stack, idx = w
        nd = stack.ndim - 1
        return pl.BlockSpec((None,) + stack.shape[1:], lambda *_: (idx,) + (0,) * nd, pipeline_mode=pl.Buffered(1))
    return _const_spec(w.shape)


def _weight_arrays(ws):
    return [w[0] if isinstance(w, tuple) else w for w in ws]


def _params(n_grid):
    return pltpu.CompilerParams(dimension_semantics=("arbitrary",) * n_grid,
                                vmem_limit_bytes=VMEM_LIMIT)


def _proj_even_kernel(h_ref, g_ref, win_ref, qg_ref, kvg_ref, wq_ref, wqr_ref, wkv_ref,
                      ta_ref, tb_ref, qa_ref, ka_ref, va_ref, qb_ref, kb_ref, vb_ref,
                      qr_ref, kr_ref, vr_ref, stage_ref, mid_ref):
    xn = _rms(h_ref[...], g_ref[...]).astype(BF)
    z = _dot(xn, win_ref[...])
    cq = _rms(z[:, 0:A_Q_RANK], qg_ref[...]).astype(BF)
    ckv = _rms(z[:, A_Q_RANK:COL_QB], kvg_ref[...]).astype(BF)
    qa = _dot(cq, wq_ref[...])
    qr = _dot(cq, wqr_ref[...])
    cq_t, sq_t = ta_ref[:, 0:LANES], ta_ref[:, LANES:2 * LANES]
    ck_t, sk_t = ta_ref[:, 2 * LANES:3 * LANES], ta_ref[:, 3 * LANES:4 * LANES]
    for hd in range(A_HEADS):
        sl = slice(hd * LANES, (hd + 1) * LANES)
        qa_ref[:, sl] = (qa[:, sl] * cq_t + qr[:, sl] * sq_t).astype(BF)
    kpe = z[:, COL_KPE:COL_KPE + LANES] * ck_t + z[:, COL_KPE_ROT:COL_KPE_ROT + LANES] * sk_t
    kv = _dot(jnp.concatenate([ckv, kpe.astype(BF)], axis=1), wkv_ref[...])
    ka_ref[...] = kv[:, 0:A_PAD].astype(BF)
    col = lax.broadcasted_iota(jnp.int32, (1, A_PAD), 1)
    own_half = ((col // HEAD_DIM) % 2) == ((col // LANES) % 2)
    va_ref[...] = (kv[:, A_PAD:] + jnp.where(own_half, 0.0, 1.0)).astype(BF)
    for p in range(B_WIDTH // LANES):
        for idx, (col, out) in enumerate(((COL_QB, qb_ref), (COL_KB, kb_ref))):
            t0 = (1 - idx) * 3 * LANES
            c_t, s_hi, s_lo = (tb_ref[:, t0 + n * LANES:t0 + (n + 1) * LANES] for n in range(3))
            xs = z[:, col + p * LANES: col + (p + 1) * LANES]
            r = (xs * c_t + pltpu.roll(xs, HEAD_DIM // 2, 1) * s_hi
                 + pltpu.roll(xs, LANES - HEAD_DIM // 2, 1) * s_lo)
            out[:, p * LANES:(p + 1) * LANES] = r.astype(BF)
            stage_ref[idx, p] = r
        stage_ref[2, p] = z[:, COL_VB + p * LANES:COL_VB + (p + 1) * LANES]
    vb_ref[...] = z[:, COL_VB:COL_KPE].astype(BF)
    tm = stage_ref.shape[2]
    for idx, out in enumerate((qr_ref, kr_ref, vr_ref)):
        for p in range(B_WIDTH // LANES):
            for b in range(DEINT):
                mid_ref[idx, p, b] = stage_ref[idx, p, pl.ds(b, tm // DEINT, stride=DEINT), :]
            for r in range(RES_GROUPS):
                out[r, :, p * LANES:(p + 1) * LANES] = (
                    mid_ref[idx, p, r % DEINT, pl.ds(r // DEINT, tm // RES_GROUPS, stride=DEINT), :].astype(BF))


def _proj_even(h, g, w, layer, ta, tb, seq, tm=512):
    T = h.shape[0]
    nseq = seq // tm
    row = lambda c: pl.BlockSpec((tm, c), lambda i: (i, 0))
    tab = lambda c: pl.BlockSpec((tm, c), lambda i: (i % nseq, 0))
    consts = [g] + [(w[name], layer) for name in ("win", "qg", "kvg", "wq", "wqr", "wkv")]
    out_cols = (A_PAD, A_PAD, A_PAD, B_WIDTH, B_WIDTH, B_WIDTH)
    grouped = pl.BlockSpec((None, RES_GROUPS, tm // RES_GROUPS, B_WIDTH), lambda i: (i // nseq, 0, i % nseq, 0))
    grouped_shape = jax.ShapeDtypeStruct((T // seq, RES_GROUPS, seq // RES_GROUPS, B_WIDTH), BF)
    return pl.pallas_call(
        _proj_even_kernel,
        grid=(T // tm,),
        in_specs=[row(D_MODEL)] + [_weight_spec(c) for c in consts] + [tab(4 * LANES), tab(6 * LANES)],
        out_specs=[row(c) for c in out_cols] + [grouped] * 3,
        out_shape=[jax.ShapeDtypeStruct((T, c), BF) for c in out_cols] + [grouped_shape] * 3,
        scratch_shapes=[pltpu.VMEM((3, B_WIDTH // LANES, tm, LANES), F32),
                        pltpu.VMEM((3, B_WIDTH // LANES, DEINT, tm // DEINT, LANES), F32)],
        compiler_params=_params(1),
        name="proj_even",
    )(h, *_weight_arrays(consts), ta, tb)


def _mla_kernel(q_ref, k_ref, v_ref, o_ref):
    low = _low_half(q_ref.shape[0])
    tile = lambda hd: slice(hd * LANES, (hd + 1) * LANES)
    for g in range(A_HEADS // MLA_GROUP):
        heads = range(g * MLA_GROUP, (g + 1) * MLA_GROUP)
        scores = [_dot_nt(q_ref[:, tile(hd)], k_ref[:, tile(hd)]) for hd in heads]
        probs = [_exp_rows(s)[0] for s in scores]
        outs = [_dot(p, v_ref[:, tile(hd)]) for p, hd in zip(probs, heads)]
        for pair in range(MLA_GROUP // 2):
            o_ref[:, tile(g * MLA_GROUP // 2 + pair)] = _pair_finish(outs[2 * pair], outs[2 * pair + 1], low).astype(BF)


def _mla(qa, ka, va, batch, seq, tq=1024):
    q3, k3, v3 = (t.reshape(batch, seq, A_PAD) for t in (qa, ka, va))
    out = pl.pallas_call(
        _mla_kernel,
        grid=(batch, seq // tq),
        in_specs=[pl.BlockSpec((None, tq, A_PAD), lambda b, i: (b, i, 0)),
                  pl.BlockSpec((None, seq, A_PAD), lambda b, i: (b, 0, 0)),
                  pl.BlockSpec((None, seq, A_PAD), lambda b, i: (b, 0, 0))],
        out_specs=pl.BlockSpec((None, tq, B_WIDTH), lambda b, i: (b, i, 0)),
        out_shape=jax.ShapeDtypeStruct((batch, seq, A_HEADS * A_V_DIM), BF),
        compiler_params=_params(2),
        name="mla_attention",
    )(q3, k3, v3)
    return out.reshape(batch * seq, A_HEADS * A_V_DIM)


def _low_half(n):
    return lax.broadcasted_iota(jnp.int32, (n, LANES), 1) < HEAD_DIM


def _pair_scores(q2, k2, bias, low_q):
    zero = jnp.zeros_like(q2)
    return [_dot_nt(jnp.where(low_q, q2, zero), k2) + bias, _dot_nt(jnp.where(low_q, zero, q2), k2) + bias]


def _pair_values(v2, low_k):
    one = jnp.ones_like(v2)
    return [jnp.where(low_k, v2, one), jnp.where(low_k, one, v2)]


def _exp_rows(s):
    m = jnp.max(s, axis=-1, keepdims=True)
    return jnp.exp2(s - m).astype(BF), m


def _pair_finish(o0, o1, low_q, m0=None, m1=None):
    den = pltpu.roll(jnp.where(low_q, o1, o0), HEAD_DIM, 1)
    out = jnp.where(low_q, o0, o1) / den
    if m0 is None:
        return out
    return out, jnp.where(low_q, m0, m1) * LN2 + jnp.log(den)


def _attend_blocks(blocks, low_q, low_k):
    scores = [_pair_scores(q2, k2, bias, low_q) for q2, k2, _, bias in blocks]
    probs = [[_exp_rows(s) for s in pair] for pair in scores]
    results = []
    for (_, _, v2, _), pair in zip(blocks, probs):
        vals = _pair_values(v2, low_k)
        o0, o1 = _dot(pair[0][0], vals[0]), _dot(pair[1][0], vals[1])
        results.append(_pair_finish(o0, o1, low_q, pair[0][1], pair[1][1]))
    return results


def _grouped_branch(qg_ref, kg_ref, vg_ref, mask_ref, mask_base, o_ref, l_ref, dil):
    n16 = qg_ref.shape[1]
    per, jq, jk, n_jb = _grouped_geometry(dil, n16)
    nq, nk = per * jq, per * jk
    low_q, low_k = _low_half(nq), _low_half(nk)
    r_unroll = max(1, DIL_UNROLL // n_jb)

    def body(i, carry):
        blocks, where = [], []
        for ru in range(r_unroll):
            rr = i * r_unroll + ru
            for jb in range(n_jb):
                q0, k0 = _grouped_window(jb, per, jq, jk, n16)
                cat = lambda ref, lo, n: jnp.concatenate(
                    [ref[rr + dil * u, lo:lo + n, :] for u in range(per)], axis=0)
                blocks.append((cat(qg_ref, q0, jq), cat(kg_ref, k0, jk), cat(vg_ref, k0, jk),
                               mask_ref[mask_base + jb, :, 0:nk]))
                where.append((rr, q0))
        for (rr, q0), (o, l) in zip(where, _attend_blocks(blocks, low_q, low_k)):
            for u in range(per):
                o_ref[rr + dil * u, q0:q0 + jq, :] = o[u * jq:(u + 1) * jq]
                l_ref[rr + dil * u, q0:q0 + jq, :] = l[u * jq:(u + 1) * jq]
        return carry

    _loop(dil // r_unroll, body)


def _grouped_geometry(dil, n16):
    per = RES_GROUPS // dil
    jq = DIL_QB // per
    jk = min(DIL_KB, n16 * per) // per
    return per, jq, jk, n16 // jq


def _grouped_window(jb, per, jq, jk, n16):
    q0 = jq * jb
    return q0, int(np.clip(q0 - HALF_WINDOW // per, 0, n16 - jk))


def _dilated_masks(seq):
    n16 = seq // RES_GROUPS
    neg = np.float32(NEG_INF)
    q = np.arange(DIL_QB)[:, None]
    k = np.arange(DIL_KB)[None, :]
    tiles = []
    n_blocks = seq // DIL_QB
    for a in DIL1_MASK_BLOCKS:
        a %= n_blocks
        qs = a * DIL_QB
        ks = int(np.clip(qs - HALF_WINDOW, 0, seq - DIL_KB))
        tiles.append(np.where(np.abs((k + ks) - (q + qs)) <= HALF_WINDOW, 0, neg))
    for _, dil in B_BRANCHES[1:]:
        per, jq, jk, n_jb = _grouped_geometry(dil, n16)
        pos = lambda i, n: per * (i % n) + i // n
        for jb in range(n_jb):
            q0, k0 = _grouped_window(jb, per, jq, jk, n16)
            rel = (pos(k, jk) + per * k0) - (pos(q, jq) + per * q0)
            tile = np.where(np.abs(rel) <= HALF_WINDOW, 0, neg)
            tile[:, per * jk:] = neg
            tiles.append(tile)
    return jnp.asarray(np.stack(tiles).astype(np.float32))


def _dilated_kernel(qn_ref, kn_ref, vn_ref, qg_ref, kg_ref, vg_ref, mask_ref, out_ref,
                    o1_ref, l1_ref, o2_ref, l2_ref, o3_ref, l3_ref, mix_ref, om_ref, lm_ref, mm_ref):
    seq = qn_ref.shape[0]
    n16 = seq // RES_GROUPS
    n_blocks = seq // DIL_QB
    low_q, low_k = _low_half(DIL_QB), _low_half(DIL_KB)

    def body(i, carry):
        blocks, starts = [], []
        for u in range(DIL_UNROLL):
            a = i * DIL_UNROLL + u
            qs = _aligned(a * DIL_QB, DIL_QB)
            ks = _aligned(_clip(a * DIL_QB - HALF_WINDOW, 0, seq - DIL_KB), HALF_WINDOW)
            variant = _select(a == 0, 0, _select(a == n_blocks - 1, 2, 1))
            blocks.append((qn_ref[pl.ds(qs, DIL_QB), :], kn_ref[pl.ds(ks, DIL_KB), :],
                           vn_ref[pl.ds(ks, DIL_KB), :], mask_ref[variant]))
            starts.append(qs)
        for qs, (o, l) in zip(starts, _attend_blocks(blocks, low_q, low_k)):
            o1_ref[pl.ds(qs, DIL_QB), :] = o
            l1_ref[pl.ds(qs, DIL_QB), :] = l
        return carry

    _loop(n_blocks // DIL_UNROLL, body)
    base4 = len(DIL1_MASK_BLOCKS)
    base16 = base4 + _grouped_geometry(B_BRANCHES[1][1], n16)[3]
    _grouped_branch(qg_ref, kg_ref, vg_ref, mask_ref, base4, o2_ref, l2_ref, B_BRANCHES[1][1])
    _grouped_branch(qg_ref, kg_ref, vg_ref, mask_ref, base16, o3_ref, l3_ref, B_BRANCHES[2][1])
    for b in range(DEINT):
        quarter = pl.ds(b, seq // DEINT, stride=DEINT)
        om_ref[b] = o1_ref[quarter, :]
        lm_ref[b] = l1_ref[quarter, :]
    for r in range(RES_GROUPS):
        rows = pl.ds(r // DEINT, n16, stride=DEINT)
        la, lb, lc = lm_ref[r % DEINT, rows, :], l2_ref[r], l3_ref[r]
        m = jnp.maximum(jnp.maximum(la, lb), lc)
        ea, eb, ec = jnp.exp(la - m), jnp.exp(lb - m), jnp.exp(lc - m)
        mm_ref[r % DEINT, rows, :] = (ea * om_ref[r % DEINT, rows, :] + eb * o2_ref[r] + ec * o3_ref[r]) / (ea + eb + ec)
    for b in range(DEINT):
        mix_ref[pl.ds(b, seq // DEINT, stride=DEINT), :] = mm_ref[b]
    out_ref[...] = mix_ref[...].astype(BF)


def _dilated(qn, kn, vn, qg, kg, vg, batch, seq):
    n16 = seq // RES_GROUPS
    nat = pl.BlockSpec((None, seq, LANES), lambda b, hp: (b, 0, hp))
    grp = pl.BlockSpec((None, RES_GROUPS, n16, LANES), lambda b, hp: (b, 0, 0, hp))
    nat3 = lambda t: t.reshape(batch, seq, B_WIDTH)
    masks = _dilated_masks(seq)
    out = pl.pallas_call(
        _dilated_kernel,
        grid=(batch, B_WIDTH // LANES),
        in_specs=[nat, nat, nat, grp, grp, grp, _const_spec(masks.shape)],
        out_specs=nat,
        out_shape=jax.ShapeDtypeStruct((batch, seq, B_WIDTH), BF),
        scratch_shapes=[pltpu.VMEM((seq, LANES), F32), pltpu.VMEM((seq, LANES), F32)]
        + [pltpu.VMEM((RES_GROUPS, n16, LANES), F32)] * 4 + [pltpu.VMEM((seq, LANES), F32)]
        + [pltpu.VMEM((DEINT, seq // DEINT, LANES), F32)] * 3,
        compiler_params=_params(2),
        name="dilated_attention",
    )(nat3(qn), nat3(kn), nat3(vn), qg, kg, vg, masks)
    return out.reshape(batch * seq, B_WIDTH)


def _proj_odd_kernel(h_ref, g_ref, w_ref, o_ref):
    xn = _rms(h_ref[...], g_ref[...]).astype(BF)
    z = _dot(xn, w_ref[...])
    o_ref[:, 0:C_WIDTH] = (z[:, 0:C_WIDTH] * LOG2E).astype(BF)
    o_ref[:, C_WIDTH:] = z[:, C_WIDTH:].astype(BF)


def _proj_odd(h, g, w, tm=512):
    T = h.shape[0]
    n = w[0].shape[-1]
    return pl.pallas_call(
        _proj_odd_kernel,
        grid=(T // tm,),
        in_specs=[pl.BlockSpec((tm, D_MODEL), lambda i: (i, 0)), _weight_spec(g), _weight_spec(w)],
        out_specs=pl.BlockSpec((tm, n), lambda i: (i, 0)),
        out_shape=jax.ShapeDtypeStruct((T, n), BF),
        compiler_params=_params(1),
        name="proj_odd",
    )(h, *_weight_arrays([g, w]))


def _na_kernel(q_ref, k_ref, v_ref, slab_ref, o_ref, vm_ref, b_ref, *, seq):
    rows = seq // GRID_W
    nq = NA_QROWS * GRID_W
    nk = NA_KROWS * GRID_W
    low = _low_half(nq)

    @pl.when(pl.program_id(1) == 0)
    def _():
        low_c = _low_half(GRID_W)
        index = _na_slab_index(seq)
        for v, qr, kp in np.ndindex(NA_VARIANTS, NA_QROWS, NA_KROWS // 2):
            i0, i1 = int(index[v, qr, 2 * kp]), int(index[v, qr, 2 * kp + 1])
            for j in range(2):
                b_ref[j, v, qr * GRID_W:(qr + 1) * GRID_W, kp * LANES:(kp + 1) * LANES] = jnp.where(
                    low_c, slab_ref[j, i0], slab_ref[j, i1])

    vals = _pair_values(v_ref[...], _low_half(seq))
    vm_ref[0] = vals[0]
    vm_ref[1] = vals[1]
    n_blocks = rows // NA_QROWS

    def body(i, carry):
        scores, where = [], []
        for u in range(NA_UNROLL):
            a = i * NA_UNROLL + u
            qs = _aligned(a * nq, nq)
            ks = _aligned(_clip(NA_QROWS * a - NA_ROWS // 2, 0, rows - NA_KROWS) * GRID_W, GRID_W)
            var = _select(a < 2, a, _select(a >= n_blocks - 2, a - (n_blocks - NA_VARIANTS), 2))
            q2 = q_ref[pl.ds(qs, nq), :]
            k2 = k_ref[pl.ds(ks, nk), :]
            zero = jnp.zeros_like(q2)
            scores.append([_dot_nt(jnp.where(low, q2, zero), k2) + b_ref[0, var],
                           _dot_nt(jnp.where(low, zero, q2), k2) + b_ref[1, var]])
            where.append((qs, ks))
        probs = [[_exp_rows(s)[0] for s in pair] for pair in scores]
        for (qs, ks), pair in zip(where, probs):
            o0 = _dot(pair[0], vm_ref[0, pl.ds(ks, nk), :])
            o1 = _dot(pair[1], vm_ref[1, pl.ds(ks, nk), :])
            o_ref[pl.ds(qs, nq), :] = _pair_finish(o0, o1, low).astype(BF)
        return carry

    _loop(n_blocks // NA_UNROLL, body)


def _na_slab_index(seq):
    rows = seq // GRID_W
    n_blocks = rows // NA_QROWS
    n_off = 2 * NA_ROWS - 1
    idx = np.full((NA_VARIANTS, NA_QROWS, NA_KROWS), n_off, np.int32)
    for v, a in enumerate((0, 1, 2, n_blocks - 2, n_blocks - 1)):
        ws = int(np.clip(NA_QROWS * a - NA_ROWS // 2, 0, rows - NA_KROWS))
        for qr in range(NA_QROWS):
            r = NA_QROWS * a + qr
            rs = int(np.clip(r - NA_ROWS // 2, 0, rows - NA_ROWS))
            for kr in range(NA_KROWS):
                krow = ws + kr
                if rs <= krow < rs + NA_ROWS:
                    idx[v, qr, kr] = krow - r + NA_ROWS - 1
    return idx


def _na_bias_slabs(rpb):
    col = np.arange(GRID_W)
    cs = np.clip(col - NA_COLS // 2, 0, GRID_W - NA_COLS)
    col_valid = (col[None, :] >= cs[:, None]) & (col[None, :] < cs[:, None] + NA_COLS)
    col_off = np.clip(col[None, :] - col[:, None] + NA_COLS - 1, 0, 2 * NA_COLS - 2)
    onehot = (col_off[..., None] == np.arange(2 * NA_COLS - 1)).astype(np.float32)
    toeplitz = jnp.einsum("hrd,ckd->hrck", rpb, onehot, precision=lax.Precision.HIGHEST)
    t2 = jnp.where(col_valid[None, None], toeplitz * LOG2E, NEG_INF)
    t2 = jnp.concatenate([t2, jnp.full_like(t2[:, :1], NEG_INF)], axis=1)
    return jnp.concatenate([t2, t2], axis=-1)


def _neighbourhood(qkv, slabs, layer, batch, seq):
    qkv3 = qkv.reshape(batch, seq, 3 * C_WIDTH)
    n_hp = C_WIDTH // LANES
    blk = lambda off: pl.BlockSpec((None, seq, LANES), lambda hp, b: (b, 0, off + hp))
    out = pl.pallas_call(
        functools.partial(_na_kernel, seq=seq),
        grid=(n_hp, batch),
        in_specs=[blk(0), blk(n_hp), blk(2 * n_hp),
                  pl.BlockSpec((None, 2) + slabs.shape[2:], lambda hp, b: (layer, hp, 0, 0, 0))],
        out_specs=pl.BlockSpec((None, seq, LANES), lambda hp, b: (b, 0, hp)),
        out_shape=jax.ShapeDtypeStruct((batch, seq, C_WIDTH), BF),
        scratch_shapes=[pltpu.VMEM((2, seq, LANES), BF),
                        pltpu.VMEM((2, NA_VARIANTS, NA_QROWS * GRID_W, NA_KROWS * GRID_W), F32)],
        compiler_params=_params(2),
        name="neighbourhood_attention",
    )(qkv3, qkv3, qkv3, slabs)
    return out.reshape(batch * seq, C_WIDTH)


def _post_kernel(*refs, n_attn, final):
    h_ref = refs[0]
    attn = refs[1:1 + n_attn]
    wo_ref, g_ref, wg_ref, wu_ref, wd_ref = refs[1 + n_attn:6 + n_attn]
    fg_ref = refs[6 + n_attn] if final else None
    out_ref = refs[-1]
    h1 = h_ref[...]
    off = 0
    for a in attn:
        width = a.shape[1]
        h1 = h1 + _dot(a[...], wo_ref[off:off + width, :])
        off += width
    xn = _rms(h1, g_ref[...]).astype(BF)
    gate = _dot(xn, wg_ref[...])
    up = _dot(xn, wu_ref[...])
    act = (gate * (1.0 / (1.0 + jnp.exp(-gate))) * up).astype(BF)
    y = h1 + _dot(act, wd_ref[...])
    if final:
        y = _rms(y, fg_ref[...])
    out_ref[...] = y


def _post(h, attn, wo, g, wg, wu, wd, final_g=None, tm=512):
    T = h.shape[0]
    final = final_g is not None
    row = lambda c: pl.BlockSpec((tm, c), lambda i: (i, 0))
    consts = [wo, g, wg, wu, wd] + ([final_g] if final else [])
    return pl.pallas_call(
        functools.partial(_post_kernel, n_attn=len(attn), final=final),
        grid=(T // tm,),
        in_specs=[row(D_MODEL)] + [row(a.shape[1]) for a in attn] + [_weight_spec(c) for c in consts],
        out_specs=row(D_MODEL),
        out_shape=jax.ShapeDtypeStruct((T, D_MODEL), F32),
        compiler_params=_params(1),
        name="outproj_ffn",
    )(h, *attn, *_weight_arrays(consts))


def _rope_tables(seq):
    pos = jnp.arange(seq, dtype=F32)

    def cs(dim):
        inv = ROPE_THETA ** (-jnp.arange(0, dim, 2, dtype=F32) / dim)
        ang = pos[:, None] * inv[None, :]
        return jnp.cos(ang), jnp.sin(ang)

    cos_a, sin_a = cs(A_ROPE_DIM)
    cos2, sin2 = jnp.tile(cos_a, (1, 2)), jnp.tile(sin_a, (1, 2))
    zeros = lambda n: jnp.zeros((seq, n), F32)
    scale = (A_NOPE_DIM + A_ROPE_DIM) ** -0.5 * LOG2E
    pad = LANES - A_NOPE_DIM - A_ROPE_DIM
    cq = jnp.concatenate([jnp.full((seq, A_NOPE_DIM), scale, F32), cos2 * scale, zeros(pad)], axis=1)
    sq = jnp.concatenate([zeros(A_NOPE_DIM), sin2 * scale, zeros(pad)], axis=1)
    ck = jnp.concatenate([cos2, zeros(LANES - A_ROPE_DIM)], axis=1)
    sk = jnp.concatenate([sin2, zeros(LANES - A_ROPE_DIM)], axis=1)
    ta = jnp.concatenate([cq, sq, ck, sk], axis=1)
    cos_b, sin_b = cs(HEAD_DIM)
    zb = jnp.zeros_like(sin_b)
    c = jnp.tile(cos_b, (1, 4))
    s_hi = jnp.tile(jnp.concatenate([zb, sin_b], axis=1), (1, 2))
    s_lo = jnp.tile(jnp.concatenate([-sin_b, zb], axis=1), (1, 2))
    tb = jnp.concatenate([c, s_hi, s_lo], axis=1)
    tb = jnp.concatenate([tb, tb * LOG2E], axis=1)
    return ta, tb


def _rot_half_cols(w):
    half = w.shape[1] // 2
    return jnp.concatenate([-w[:, half:], w[:, :half]], axis=1)


def _prep_even(w_in, q_norm, w_q_up, kv_norm, w_kv_up):
    d = w_in.shape[0]
    c_q, c_kv, k_pe, qkv_b = jnp.split(w_in, [A_Q_RANK, COL_QB, COL_QB + A_ROPE_DIM], axis=1)
    q_b, k_b, v_b = jnp.split(qkv_b, 3, axis=1)
    padc = jnp.zeros((d, LANES - A_ROPE_DIM), F32)
    win = jnp.concatenate([c_q, c_kv, q_b * HEAD_DIM ** -0.5, k_b, v_b,
                           k_pe, padc, _rot_half_cols(k_pe), padc], axis=1).astype(BF)
    wq3 = w_q_up.reshape(A_Q_RANK, A_HEADS, A_NOPE_DIM + A_ROPE_DIM)
    nope, rope = wq3[..., :A_NOPE_DIM], wq3[..., A_NOPE_DIM:]
    zpad = jnp.zeros((A_Q_RANK, A_HEADS, LANES - A_NOPE_DIM - A_ROPE_DIM), F32)
    wq = jnp.concatenate([nope, rope, zpad], axis=-1).reshape(A_Q_RANK, A_PAD).astype(BF)
    rope_rot = jnp.concatenate([-rope[..., A_ROPE_DIM // 2:], rope[..., :A_ROPE_DIM // 2]], axis=-1)
    wqr = jnp.concatenate([jnp.zeros_like(nope), rope_rot, zpad], axis=-1).reshape(A_Q_RANK, A_PAD).astype(BF)
    wkv3 = w_kv_up.reshape(A_KV_RANK, A_HEADS, A_NOPE_DIM + A_V_DIM)
    k_nope, v = wkv3[..., :A_NOPE_DIM], wkv3[..., A_NOPE_DIM:]
    zk = jnp.zeros_like(k_nope)
    wk = jnp.concatenate([k_nope, zk], axis=-1).reshape(A_KV_RANK, A_PAD).astype(BF)
    v_even = jnp.concatenate([v, jnp.zeros_like(v)], axis=-1)
    v_odd = jnp.concatenate([jnp.zeros_like(v), v], axis=-1)
    odd = (np.arange(A_HEADS) % 2 == 1)[None, :, None]
    wv = jnp.where(odd, v_odd, v_even).reshape(A_KV_RANK, A_PAD).astype(BF)
    e = np.zeros((LANES, A_PAD), np.float32)
    for hd in range(A_HEADS):
        for j in range(A_ROPE_DIM):
            e[j, hd * LANES + A_NOPE_DIM + j] = 1.0
    wkv = jnp.concatenate([jnp.concatenate([wk, wv], axis=1),
                           jnp.concatenate([jnp.asarray(e, BF), jnp.zeros((LANES, A_PAD), BF)], axis=1)], axis=0)
    return dict(win=win, qg=q_norm[None, :], kvg=kv_norm[None, :], wq=wq, wqr=wqr, wkv=wkv)


def kernel(x, ev_norm, ev_w_in, ev_q_norm, ev_w_q_up, ev_kv_norm, ev_w_kv_up, ev_w_out, od_norm, od_w_qkv, od_rpb, od_w_out, ffn_norm, ffn_w_gate, ffn_w_up, ffn_w_down, final_norm):
    batch, seq, d = x.shape
    assert d == D_MODEL and seq % (DIL_QB * DIL_UNROLL) == 0 and (seq // RES_GROUPS) % DIL_QB == 0
    assert seq % (NA_QROWS * NA_UNROLL * GRID_W) == 0
    h = x.reshape(batch * seq, d)
    ta, tb = _rope_tables(seq)
    wg_all, wu_all, wd_all = ffn_w_gate.astype(BF), ffn_w_up.astype(BF), ffn_w_down.astype(BF)
    ev_wo_all, od_wo_all = ev_w_out.astype(BF), od_w_out.astype(BF)
    wqkv_all = jnp.concatenate([od_w_qkv[:, :, :C_WIDTH] * HEAD_DIM ** -0.5, od_w_qkv[:, :, C_WIDTH:]],
                               axis=2).astype(BF)
    ev_all = jax.vmap(_prep_even)(ev_w_in, ev_q_norm, ev_w_q_up, ev_kv_norm, ev_w_kv_up)
    slabs_all = jax.vmap(_na_bias_slabs)(od_rpb)
    ev_g, od_g, ffn_g = ev_norm[:, None, :], od_norm[:, None, :], ffn_norm[:, None, :]
    for layer in range(DEPTH):
        i = layer // 2
        if layer % 2 == 0:
            qa, ka, va, qb, kb, vb, qg, kg, vg = _proj_even(h, (ev_g, i), ev_all, i, ta, tb, seq)
            o_a = _mla(qa, ka, va, batch, seq)
            o_b = _dilated(qb, kb, vb, qg, kg, vg, batch, seq)
            attn, wo = [o_a, o_b], (ev_wo_all, i)
        else:
            qkv = _proj_odd(h, (od_g, i), (wqkv_all, i))
            attn = [_neighbourhood(qkv, slabs_all, i, batch, seq)]
            wo = (od_wo_all, i)
        h = _post(h, attn, wo, (ffn_g, layer), (wg_all, layer), (wu_all, layer), (wd_all, layer),
                  final_g=final_norm[None, :] if layer == DEPTH - 1 else None)
    return h.reshape(batch, seq, d)
```

```python
import functools

import numpy as np
import jax
import jax.numpy as jnp
from jax import lax
from jax.experimental import pallas as pl
from jax.experimental.pallas import tpu as pltpu

BF = jnp.bfloat16
F32 = jnp.float32

D_MODEL = 1024
DEPTH = 4
HEAD_DIM = 64
A_HEADS = 8
A_Q_RANK = 256
A_KV_RANK = 128
A_NOPE_DIM = 64
A_ROPE_DIM = 32
A_V_DIM = 64
B_HEADS = 8
B_BRANCHES = ((128, 1), (512, 4), (2048, 16))
C_HEADS = 16
GRID_W = 64
NA_ROWS = 8
NA_COLS = 16
D_FF = -(-8 * D_MODEL // (3 * 256)) * 256
ROPE_THETA = 10000.0
EPS = 1e-6
NEG_INF = -1e30
LOG2E = float(np.log2(np.e))
LN2 = float(np.log(2.0))

LANES = 128
V7X_VMEM_BYTES = 64 * 1024 * 1024
B_WIDTH = B_HEADS * HEAD_DIM
C_WIDTH = C_HEADS * HEAD_DIM
A_PAD = A_HEADS * LANES
COL_QB = A_Q_RANK + A_KV_RANK
COL_KB = COL_QB + B_WIDTH
COL_VB = COL_KB + B_WIDTH
COL_KPE = COL_VB + B_WIDTH
COL_KPE_ROT = COL_KPE + LANES
HALF_WINDOW = B_BRANCHES[0][0] // (2 * B_BRANCHES[0][1])
assert all(w // (2 * d) == HALF_WINDOW for w, d in B_BRANCHES)
NA_QROWS = 2
NA_KROWS = NA_QROWS + NA_ROWS
NA_VARIANTS = 5
NA_UNROLL = 16
RES_GROUPS = B_BRANCHES[-1][1]
DEINT = 4
assert DEINT * DEINT == RES_GROUPS
DIL_QB = 2 * HALF_WINDOW
DIL_KB = DIL_QB + 2 * HALF_WINDOW
DIL_UNROLL = 16
DIL1_MASK_BLOCKS = (0, 1, -1)
MLA_GROUP = 2
POST_SPLIT = 2
VMEM_LIMIT = V7X_VMEM_BYTES - 8 * 1024 * 1024


def _rms(x, g):
    return x * lax.rsqrt(jnp.mean(x * x, axis=-1, keepdims=True) + EPS) * g


def _dot(a, b):
    return jnp.dot(a, b, preferred_element_type=F32)


def _dot_nt(a, b):
    return lax.dot_general(a, b, (((1,), (1,)), ((), ())), preferred_element_type=F32)


def _loop(trips, body):
    if trips == 1:
        body(0, 0)
    else:
        lax.fori_loop(0, trips, body, 0)


def _clip(x, lo, hi):
    return min(max(x, lo), hi) if isinstance(x, int) else jnp.clip(x, lo, hi)


def _select(pred, a, b):
    return (a if pred else b) if isinstance(pred, bool) else jnp.where(pred, a, b)


def _aligned(x, m):
    return x if isinstance(x, int) else pl.multiple_of(x, m)


def _const_spec(shape):
    nd = len(shape)
    return pl.BlockSpec(shape, lambda *_: (0,) * nd, pipeline_mode=pl.Buffered(1))


def _weight_spec(w):
    if isinstance(w, tuple):
        stack, idx = w
        nd = stack.ndim - 1
        return pl.BlockSpec((None,) + stack.shape[1:], lambda *_: (idx,) + (0,) * nd, pipeline_mode=pl.Buffered(1))
    return _const_spec(w.shape)


def _weight_arrays(ws):
    return [w[0] if isinstance(w, tuple) else w for w in ws]


def _params(n_grid):
    return pltpu.CompilerParams(dimension_semantics=("arbitrary",) * n_grid,
                                vmem_limit_bytes=VMEM_LIMIT)


def _proj_even_kernel(h_ref, g_ref, win_ref, qg_ref, kvg_ref, wq_ref, wqr_ref, wkv_ref,
                      ta_ref, tb_ref, qa_ref, ka_ref, va_ref, qb_ref, kb_ref, vb_ref,
                      qr_ref, kr_ref, vr_ref, stage_ref, mid_ref):
    xn = _rms(h_ref[...], g_ref[...]).astype(BF)
    z = _dot(xn, win_ref[...])
    cq = _rms(z[:, 0:A_Q_RANK], qg_ref[...]).astype(BF)
    ckv = _rms(z[:, A_Q_RANK:COL_QB], kvg_ref[...]).astype(BF)
    qa = _dot(cq, wq_ref[...])
    qr = _dot(cq, wqr_ref[...])
    cq_t, sq_t = ta_ref[:, 0:LANES], ta_ref[:, LANES:2 * LANES]
    ck_t, sk_t = ta_ref[:, 2 * LANES:3 * LANES], ta_ref[:, 3 * LANES:4 * LANES]
    for hd in range(A_HEADS):
        sl = slice(hd * LANES, (hd + 1) * LANES)
        qa_ref[:, sl] = (qa[:, sl] * cq_t + qr[:, sl] * sq_t).astype(BF)
    kpe = z[:, COL_KPE:COL_KPE + LANES] * ck_t + z[:, COL_KPE_ROT:COL_KPE_ROT + LANES] * sk_t
    kv = _dot(jnp.concatenate([ckv, kpe.astype(BF)], axis=1), wkv_ref[...])
    ka_ref[...] = kv[:, 0:A_PAD].astype(BF)
    col = lax.broadcasted_iota(jnp.int32, (1, A_PAD), 1)
    own_half = ((col // HEAD_DIM) % 2) == ((col // LANES) % 2)
    va_ref[...] = (kv[:, A_PAD:] + jnp.where(own_half, 0.0, 1.0)).astype(BF)
    for p in range(B_WIDTH // LANES):
        for idx, (col, out) in enumerate(((COL_QB, qb_ref), (COL_KB, kb_ref))):
            t0 = (1 - idx) * 3 * LANES
            c_t, s_hi, s_lo = (tb_ref[:, t0 + n * LANES:t0 + (n + 1) * LANES] for n in range(3))
            xs = z[:, col + p * LANES: col + (p + 1) * LANES]
            r = (xs * c_t + pltpu.roll(xs, HEAD_DIM // 2, 1) * s_hi
                 + pltpu.roll(xs, LANES - HEAD_DIM // 2, 1) * s_lo)
            out[:, p * LANES:(p + 1) * LANES] = r.astype(BF)
            stage_ref[idx, p] = r
        stage_ref[2, p] = z[:, COL_VB + p * LANES:COL_VB + (p + 1) * LANES]
    vb_ref[...] = z[:, COL_VB:COL_KPE].astype(BF)
    tm = stage_ref.shape[2]
    for idx, out in enumerate((qr_ref, kr_ref, vr_ref)):
        for p in range(B_WIDTH // LANES):
            for b in range(DEINT):
                mid_ref[idx, p, b] = stage_ref[idx, p, pl.ds(b, tm // DEINT, stride=DEINT), :]
            for r in range(RES_GROUPS):
                out[r, :, p * LANES:(p + 1) * LANES] = (
                    mid_ref[idx, p, r % DEINT, pl.ds(r // DEINT, tm // RES_GROUPS, stride=DEINT), :].astype(BF))


def _proj_even(h, g, w, layer, ta, tb, seq, tm=512):
    T = h.shape[0]
    nseq = seq // tm
    row = lambda c: pl.BlockSpec((tm, c), lambda i: (i, 0))
    tab = lambda c: pl.BlockSpec((tm, c), lambda i: (i % nseq, 0))
    consts = [g] + [(w[name], layer) for name in ("win", "qg", "kvg", "wq", "wqr", "wkv")]
    out_cols = (A_PAD, A_PAD, A_PAD, B_WIDTH, B_WIDTH, B_WIDTH)
    grouped = pl.BlockSpec((None, RES_GROUPS, tm // RES_GROUPS, B_WIDTH), lambda i: (i // nseq, 0, i % nseq, 0))
    grouped_shape = jax.ShapeDtypeStruct((T // seq, RES_GROUPS, seq // RES_GROUPS, B_WIDTH), BF)
    return pl.pallas_call(
        _proj_even_kernel,
        grid=(T // tm,),
        in_specs=[row(D_MODEL)] + [_weight_spec(c) for c in consts] + [tab(4 * LANES), tab(6 * LANES)],
        out_specs=[row(c) for c in out_cols] + [grouped] * 3,
        out_shape=[jax.ShapeDtypeStruct((T, c), BF) for c in out_cols] + [grouped_shape] * 3,
        scratch_shapes=[pltpu.VMEM((3, B_WIDTH // LANES, tm, LANES), F32),
                        pltpu.VMEM((3, B_WIDTH // LANES, DEINT, tm // DEINT, LANES), F32)],
        compiler_params=_params(1),
        name="proj_even",
    )(h, *_weight_arrays(consts), ta, tb)


def _mla_kernel(q_ref, k_ref, v_ref, o_ref):
    low = _low_half(q_ref.shape[0])
    tile = lambda hd: slice(hd * LANES, (hd + 1) * LANES)
    for g in range(A_HEADS // MLA_GROUP):
        heads = range(g * MLA_GROUP, (g + 1) * MLA_GROUP)
        scores = [_dot_nt(q_ref[:, tile(hd)], k_ref[:, tile(hd)]) for hd in heads]
        probs = [_exp_rows(s)[0] for s in scores]
        outs = [_dot(p, v_ref[:, tile(hd)]) for p, hd in zip(probs, heads)]
        for pair in range(MLA_GROUP // 2):
            o_ref[:, tile(g * MLA_GROUP // 2 + pair)] = _pair_finish(outs[2 * pair], outs[2 * pair + 1], low).astype(BF)


def _mla(qa, ka, va, batch, seq, tq=1024):
    q3, k3, v3 = (t.reshape(batch, seq, A_PAD) for t in (qa, ka, va))
    out = pl.pallas_call(
        _mla_kernel,
        grid=(batch, seq // tq),
        in_specs=[pl.BlockSpec((None, tq, A_PAD), lambda b, i: (b, i, 0)),
                  pl.BlockSpec((None, seq, A_PAD), lambda b, i: (b, 0, 0)),
                  pl.BlockSpec((None, seq, A_PAD), lambda b, i: (b, 0, 0))],
        out_specs=pl.BlockSpec((None, tq, B_WIDTH), lambda b, i: (b, i, 0)),
        out_shape=jax.ShapeDtypeStruct((batch, seq, A_HEADS * A_V_DIM), BF),
        compiler_params=_params(2),
        name="mla_attention",
    )(q3, k3, v3)
    return out.reshape(batch * seq, A_HEADS * A_V_DIM)


def _low_half(n):
    return lax.broadcasted_iota(jnp.int32, (n, LANES), 1) < HEAD_DIM


def _pair_scores(q2, k2, bias, low_q):
    zero = jnp.zeros_like(q2)
    return [_dot_nt(jnp.where(low_q, q2, zero), k2) + bias, _dot_nt(jnp.where(low_q, zero, q2), k2) + bias]


def _pair_values(v2, low_k):
    one = jnp.ones_like(v2)
    return [jnp.where(low_k, v2, one), jnp.where(low_k, one, v2)]


def _exp_rows(s):
    m = jnp.max(s, axis=-1, keepdims=True)
    return jnp.exp2(s - m).astype(BF), m


def _pair_finish(o0, o1, low_q, m0=None, m1=None):
    den = pltpu.roll(jnp.where(low_q, o1, o0), HEAD_DIM, 1)
    out = jnp.where(low_q, o0, o1) / den
    if m0 is None:
        return out
    return out, jnp.where(low_q, m0, m1) * LN2 + jnp.log(den)


def _attend_blocks(blocks, low_q, low_k):
    scores = [_pair_scores(q2, k2, bias, low_q) for q2, k2, _, bias in blocks]
    probs = [[_exp_rows(s) for s in pair] for pair in scores]
    results = []
    for (_, _, v2, _), pair in zip(blocks, probs):
        vals = _pair_values(v2, low_k)
        o0, o1 = _dot(pair[0][0], vals[0]), _dot(pair[1][0], vals[1])
        results.append(_pair_finish(o0, o1, low_q, pair[0][1], pair[1][1]))
    return results


def _grouped_branch(qg_ref, kg_ref, vg_ref, mask_ref, mask_base, o_ref, l_ref, dil):
    n16 = qg_ref.shape[1]
    per, jq, jk, n_jb = _grouped_geometry(dil, n16)
    nq, nk = per * jq, per * jk
    low_q, low_k = _low_half(nq), _low_half(nk)
    r_unroll = max(1, DIL_UNROLL // n_jb)

    def body(i, carry):
        blocks, where = [], []
        for ru in range(r_unroll):
            rr = i * r_unroll + ru
            for jb in range(n_jb):
                q0, k0 = _grouped_window(jb, per, jq, jk, n16)
                cat = lambda ref, lo, n: jnp.concatenate(
                    [ref[rr + dil * u, lo:lo + n, :] for u in range(per)], axis=0)
                blocks.append((cat(qg_ref, q0, jq), cat(kg_ref, k0, jk), cat(vg_ref, k0, jk),
                               mask_ref[mask_base + jb, :, 0:nk]))
                where.append((rr, q0))
        for (rr, q0), (o, l) in zip(where, _attend_blocks(blocks, low_q, low_k)):
            for u in range(per):
                o_ref[rr + dil * u, q0:q0 + jq, :] = o[u * jq:(u + 1) * jq]
                l_ref[rr + dil * u, q0:q0 + jq, :] = l[u * jq:(u + 1) * jq]
        return carry

    _loop(dil // r_unroll, body)


def _grouped_geometry(dil, n16):
    per = RES_GROUPS // dil
    jq = DIL_QB // per
    jk = min(DIL_KB, n16 * per) // per
    return per, jq, jk, n16 // jq


def _grouped_window(jb, per, jq, jk, n16):
    q0 = jq * jb
    return q0, int(np.clip(q0 - HALF_WINDOW // per, 0, n16 - jk))


def _dilated_masks(seq):
    n16 = seq // RES_GROUPS
    neg = np.float32(NEG_INF)
    q = np.arange(DIL_QB)[:, None]
    k = np.arange(DIL_KB)[None, :]
    tiles = []
    n_blocks = seq // DIL_QB
    for a in DIL1_MASK_BLOCKS:
        a %= n_blocks
        qs = a * DIL_QB
        ks = int(np.clip(qs - HALF_WINDOW, 0, seq - DIL_KB))
        tiles.append(np.where(np.abs((k + ks) - (q + qs)) <= HALF_WINDOW, 0, neg))
    for _, dil in B_BRANCHES[1:]:
        per, jq, jk, n_jb = _grouped_geometry(dil, n16)
        pos = lambda i, n: per * (i % n) + i // n
        for jb in range(n_jb):
            q0, k0 = _grouped_window(jb, per, jq, jk, n16)
            rel = (pos(k, jk) + per * k0) - (pos(q, jq) + per * q0)
            tile = np.where(np.abs(rel) <= HALF_WINDOW, 0, neg)
            tile[:, per * jk:] = neg
            tiles.append(tile)
    return jnp.asarray(np.stack(tiles).astype(np.float32))


def _dilated_kernel(qn_ref, kn_ref, vn_ref, qg_ref, kg_ref, vg_ref, mask_ref, out_ref,
                    o1_ref, l1_ref, o2_ref, l2_ref, o3_ref, l3_ref, mix_ref, om_ref, lm_ref, mm_ref):
    seq = qn_ref.shape[0]
    n16 = seq // RES_GROUPS
    n_blocks = seq // DIL_QB
    low_q, low_k = _low_half(DIL_QB), _low_half(DIL_KB)

    def body(i, carry):
        blocks, starts = [], []
        for u in range(DIL_UNROLL):
            a = i * DIL_UNROLL + u
            qs = _aligned(a * DIL_QB, DIL_QB)
            ks = _aligned(_clip(a * DIL_QB - HALF_WINDOW, 0, seq - DIL_KB), HALF_WINDOW)
            variant = _select(a == 0, 0, _select(a == n_blocks - 1, 2, 1))
            blocks.append((qn_ref[pl.ds(qs, DIL_QB), :], kn_ref[pl.ds(ks, DIL_KB), :],
                           vn_ref[pl.ds(ks, DIL_KB), :], mask_ref[variant]))
            starts.append(qs)
        for qs, (o, l) in zip(starts, _attend_blocks(blocks, low_q, low_k)):
            o1_ref[pl.ds(qs, DIL_QB), :] = o
            l1_ref[pl.ds(qs, DIL_QB), :] = l
        return carry

    _loop(n_blocks // DIL_UNROLL, body)
    base4 = len(DIL1_MASK_BLOCKS)
    base16 = base4 + _grouped_geometry(B_BRANCHES[1][1], n16)[3]
    _grouped_branch(qg_ref, kg_ref, vg_ref, mask_ref, base4, o2_ref, l2_ref, B_BRANCHES[1][1])
    _grouped_branch(qg_ref, kg_ref, vg_ref, mask_ref, base16, o3_ref, l3_ref, B_BRANCHES[2][1])
    for b in range(DEINT):
        quarter = pl.ds(b, seq // DEINT, stride=DEINT)
        om_ref[b] = o1_ref[quarter, :]
        lm_ref[b] = l1_ref[quarter, :]
    for r in range(RES_GROUPS):
        rows = pl.ds(r // DEINT, n16, stride=DEINT)
        la, lb, lc = lm_ref[r % DEINT, rows, :], l2_ref[r], l3_ref[r]
        m = jnp.maximum(jnp.maximum(la, lb), lc)
        ea, eb, ec = jnp.exp(la - m), jnp.exp(lb - m), jnp.exp(lc - m)
        mm_ref[r % DEINT, rows, :] = (ea * om_ref[r % DEINT, rows, :] + eb * o2_ref[r] + ec * o3_ref[r]) / (ea + eb + ec)
    for b in range(DEINT):
        mix_ref[pl.ds(b, seq // DEINT, stride=DEINT), :] = mm_ref[b]
    out_ref[...] = mix_ref[...].astype(BF)


def _dilated(qn, kn, vn, qg, kg, vg, batch, seq):
    n16 = seq // RES_GROUPS
    nat = pl.BlockSpec((None, seq, LANES), lambda b, hp: (b, 0, hp))
    grp = pl.BlockSpec((None, RES_GROUPS, n16, LANES), lambda b, hp: (b, 0, 0, hp))
    nat3 = lambda t: t.reshape(batch, seq, B_WIDTH)
    masks = _dilated_masks(seq)
    out = pl.pallas_call(
        _dilated_kernel,
        grid=(batch, B_WIDTH // LANES),
        in_specs=[nat, nat, nat, grp, grp, grp, _const_spec(masks.shape)],
        out_specs=nat,
        out_shape=jax.ShapeDtypeStruct((batch, seq, B_WIDTH), BF),
        scratch_shapes=[pltpu.VMEM((seq, LANES), F32), pltpu.VMEM((seq, LANES), F32)]
        + [pltpu.VMEM((RES_GROUPS, n16, LANES), F32)] * 4 + [pltpu.VMEM((seq, LANES), F32)]
        + [pltpu.VMEM((DEINT, seq // DEINT, LANES), F32)] * 3,
        compiler_params=_params(2),
        name="dilated_attention",
    )(nat3(qn), nat3(kn), nat3(vn), qg, kg, vg, masks)
    return out.reshape(batch * seq, B_WIDTH)


def _proj_odd_kernel(h_ref, g_ref, w_ref, o_ref):
    xn = _rms(h_ref[...], g_ref[...]).astype(BF)
    z = _dot(xn, w_ref[...])
    o_ref[:, 0:C_WIDTH] = (z[:, 0:C_WIDTH] * LOG2E).astype(BF)
    o_ref[:, C_WIDTH:2 * C_WIDTH] = z[:, C_WIDTH:2 * C_WIDTH].astype(BF)
    vals = _pair_values(z[:, 2 * C_WIDTH:], (lax.broadcasted_iota(jnp.int32, (1, C_WIDTH), 1) % LANES) < HEAD_DIM)
    o_ref[:, 2 * C_WIDTH:3 * C_WIDTH] = vals[0].astype(BF)
    o_ref[:, 3 * C_WIDTH:] = vals[1].astype(BF)


def _proj_odd(h, g, w, tm=512):
    T = h.shape[0]
    n = w[0].shape[-1] + C_WIDTH
    return pl.pallas_call(
        _proj_odd_kernel,
        grid=(T // tm,),
        in_specs=[pl.BlockSpec((tm, D_MODEL), lambda i: (i, 0)), _weight_spec(g), _weight_spec(w)],
        out_specs=pl.BlockSpec((tm, n), lambda i: (i, 0)),
        out_shape=jax.ShapeDtypeStruct((T, n), BF),
        compiler_params=_params(1),
        name="proj_odd",
    )(h, *_weight_arrays([g, w]))


def _na_kernel(q_ref, k_ref, v0_ref, v1_ref, slab_ref, o_ref, b_ref, *, seq):
    rows = seq // GRID_W
    nq = NA_QROWS * GRID_W
    nk = NA_KROWS * GRID_W
    low = _low_half(nq)

    @pl.when(pl.program_id(1) == 0)
    def _():
        low_c = _low_half(GRID_W)
        index = _na_slab_index(seq)
        for v, qr, kp in np.ndindex(NA_VARIANTS, NA_QROWS, NA_KROWS // 2):
            i0, i1 = int(index[v, qr, 2 * kp]), int(index[v, qr, 2 * kp + 1])
            for j in range(2):
                b_ref[j, v, qr * GRID_W:(qr + 1) * GRID_W, kp * LANES:(kp + 1) * LANES] = jnp.where(
                    low_c, slab_ref[j, i0], slab_ref[j, i1])

    n_blocks = rows // NA_QROWS

    def body(i, carry):
        scores, where = [], []
        for u in range(NA_UNROLL):
            a = i * NA_UNROLL + u
            qs = _aligned(a * nq, nq)
            ks = _aligned(_clip(NA_QROWS * a - NA_ROWS // 2, 0, rows - NA_KROWS) * GRID_W, GRID_W)
            var = _select(a < 2, a, _select(a >= n_blocks - 2, a - (n_blocks - NA_VARIANTS), 2))
            q2 = q_ref[pl.ds(qs, nq), :]
            k2 = k_ref[pl.ds(ks, nk), :]
            zero = jnp.zeros_like(q2)
            scores.append([_dot_nt(jnp.where(low, q2, zero), k2) + b_ref[0, var],
                           _dot_nt(jnp.where(low, zero, q2), k2) + b_ref[1, var]])
            where.append((qs, ks))
        probs = [[_exp_rows(s)[0] for s in pair] for pair in scores]
        for (qs, ks), pair in zip(where, probs):
            o0 = _dot(pair[0], v0_ref[pl.ds(ks, nk), :])
            o1 = _dot(pair[1], v1_ref[pl.ds(ks, nk), :])
            o_ref[pl.ds(qs, nq), :] = _pair_finish(o0, o1, low).astype(BF)
        return carry

    _loop(n_blocks // NA_UNROLL, body)


def _na_slab_index(seq):
    rows = seq // GRID_W
    n_blocks = rows // NA_QROWS
    n_off = 2 * NA_ROWS - 1
    idx = np.full((NA_VARIANTS, NA_QROWS, NA_KROWS), n_off, np.int32)
    for v, a in enumerate((0, 1, 2, n_blocks - 2, n_blocks - 1)):
        ws = int(np.clip(NA_QROWS * a - NA_ROWS // 2, 0, rows - NA_KROWS))
        for qr in range(NA_QROWS):
            r = NA_QROWS * a + qr
            rs = int(np.clip(r - NA_ROWS // 2, 0, rows - NA_ROWS))
            for kr in range(NA_KROWS):
                krow = ws + kr
                if rs <= krow < rs + NA_ROWS:
                    idx[v, qr, kr] = krow - r + NA_ROWS - 1
    return idx


def _na_bias_slabs(rpb):
    col = np.arange(GRID_W)
    cs = np.clip(col - NA_COLS // 2, 0, GRID_W - NA_COLS)
    col_valid = (col[None, :] >= cs[:, None]) & (col[None, :] < cs[:, None] + NA_COLS)
    col_off = np.clip(col[None, :] - col[:, None] + NA_COLS - 1, 0, 2 * NA_COLS - 2)
    onehot = (col_off[..., None] == np.arange(2 * NA_COLS - 1)).astype(np.float32)
    toeplitz = jnp.einsum("hrd,ckd->hrck", rpb, onehot, precision=lax.Precision.HIGHEST)
    t2 = jnp.where(col_valid[None, None], toeplitz * LOG2E, NEG_INF)
    t2 = jnp.concatenate([t2, jnp.full_like(t2[:, :1], NEG_INF)], axis=1)
    return jnp.concatenate([t2, t2], axis=-1)


def _neighbourhood(qkv, slabs, layer, batch, seq):
    qkv3 = qkv.reshape(batch, seq, 4 * C_WIDTH)
    n_hp = C_WIDTH // LANES
    blk = lambda off: pl.BlockSpec((None, seq, LANES), lambda hp, b: (b, 0, off + hp))
    out = pl.pallas_call(
        functools.partial(_na_kernel, seq=seq),
        grid=(n_hp, batch),
        in_specs=[blk(0), blk(n_hp), blk(2 * n_hp), blk(3 * n_hp),
                  pl.BlockSpec((None, 2) + slabs.shape[2:], lambda hp, b: (layer, hp, 0, 0, 0))],
        out_specs=pl.BlockSpec((None, seq, LANES), lambda hp, b: (b, 0, hp)),
        out_shape=jax.ShapeDtypeStruct((batch, seq, C_WIDTH), BF),
        scratch_shapes=[pltpu.VMEM((2, NA_VARIANTS, NA_QROWS * GRID_W, NA_KROWS * GRID_W), F32)],
        compiler_params=_params(2),
        name="neighbourhood_attention",
    )(qkv3, qkv3, qkv3, qkv3, slabs)
    return out.reshape(batch * seq, C_WIDTH)


def _post_kernel(*refs, n_attn, final):
    h_ref = refs[0]
    attn = refs[1:1 + n_attn]
    wo_ref, g_ref, wg_ref, wu_ref, wd_ref = refs[1 + n_attn:6 + n_attn]
    fg_ref = refs[6 + n_attn] if final else None
    out_ref = refs[-1]
    tm = h_ref.shape[0]
    parts = [slice(i * tm // POST_SPLIT, (i + 1) * tm // POST_SPLIT) for i in range(POST_SPLIT)]
    h1s = []
    for rows in parts:
        h1 = h_ref[rows, :]
        off = 0
        for a in attn:
            width = a.shape[1]
            h1 = h1 + _dot(a[rows, :], wo_ref[off:off + width, :])
            off += width
        h1s.append(h1)
    xns = [_rms(h1, g_ref[...]).astype(BF) for h1 in h1s]
    gates = [_dot(xn, wg_ref[...]) for xn in xns]
    ups = [_dot(xn, wu_ref[...]) for xn in xns]
    acts = [(gate * (1.0 / (1.0 + jnp.exp(-gate))) * up).astype(BF) for gate, up in zip(gates, ups)]
    for rows, h1, act in zip(parts, h1s, acts):
        y = h1 + _dot(act, wd_ref[...])
        if final:
            y = _rms(y, fg_ref[...])
        out_ref[rows, :] = y


def _post(h, attn, wo, g, wg, wu, wd, final_g=None, tm=512):
    T = h.shape[0]
    final = final_g is not None
    row = lambda c: pl.BlockSpec((tm, c), lambda i: (i, 0))
    consts = [wo, g, wg, wu, wd] + ([final_g] if final else [])
    return pl.pallas_call(
        functools.partial(_post_kernel, n_attn=len(attn), final=final),
        grid=(T // tm,),
        in_specs=[row(D_MODEL)] + [row(a.shape[1]) for a in attn] + [_weight_spec(c) for c in consts],
        out_specs=row(D_MODEL),
        out_shape=jax.ShapeDtypeStruct((T, D_MODEL), F32),
        compiler_params=_params(1),
        name="outproj_ffn",
    )(h, *attn, *_weight_arrays(consts))


def _rope_tables(seq):
    pos = jnp.arange(seq, dtype=F32)

    def cs(dim):
        inv = ROPE_THETA ** (-jnp.arange(0, dim, 2, dtype=F32) / dim)
        ang = pos[:, None] * inv[None, :]
        return jnp.cos(ang), jnp.sin(ang)

    cos_a, sin_a = cs(A_ROPE_DIM)
    cos2, sin2 = jnp.tile(cos_a, (1, 2)), jnp.tile(sin_a, (1, 2))
    zeros = lambda n: jnp.zeros((seq, n), F32)
    scale = (A_NOPE_DIM + A_ROPE_DIM) ** -0.5 * LOG2E
    pad = LANES - A_NOPE_DIM - A_ROPE_DIM
    cq = jnp.concatenate([jnp.full((seq, A_NOPE_DIM), scale, F32), cos2 * scale, zeros(pad)], axis=1)
    sq = jnp.concatenate([zeros(A_NOPE_DIM), sin2 * scale, zeros(pad)], axis=1)
    ck = jnp.concatenate([cos2, zeros(LANES - A_ROPE_DIM)], axis=1)
    sk = jnp.concatenate([sin2, zeros(LANES - A_ROPE_DIM)], axis=1)
    ta = jnp.concatenate([cq, sq, ck, sk], axis=1)
    cos_b, sin_b = cs(HEAD_DIM)
    zb = jnp.zeros_like(sin_b)
    c = jnp.tile(cos_b, (1, 4))
    s_hi = jnp.tile(jnp.concatenate([zb, sin_b], axis=1), (1, 2))
    s_lo = jnp.tile(jnp.concatenate([-sin_b, zb], axis=1), (1, 2))
    tb = jnp.concatenate([c, s_hi, s_lo], axis=1)
    tb = jnp.concatenate([tb, tb * LOG2E], axis=1)
    return ta, tb


def _rot_half_cols(w):
    half = w.shape[1] // 2
    return jnp.concatenate([-w[:, half:], w[:, :half]], axis=1)


def _prep_even(w_in, q_norm, w_q_up, kv_norm, w_kv_up):
    d = w_in.shape[0]
    c_q, c_kv, k_pe, qkv_b = jnp.split(w_in, [A_Q_RANK, COL_QB, COL_QB + A_ROPE_DIM], axis=1)
    q_b, k_b, v_b = jnp.split(qkv_b, 3, axis=1)
    padc = jnp.zeros((d, LANES - A_ROPE_DIM), F32)
    win = jnp.concatenate([c_q, c_kv, q_b * HEAD_DIM ** -0.5, k_b, v_b,
                           k_pe, padc, _rot_half_cols(k_pe), padc], axis=1).astype(BF)
    wq3 = w_q_up.reshape(A_Q_RANK, A_HEADS, A_NOPE_DIM + A_ROPE_DIM)
    nope, rope = wq3[..., :A_NOPE_DIM], wq3[..., A_NOPE_DIM:]
    zpad = jnp.zeros((A_Q_RANK, A_HEADS, LANES - A_NOPE_DIM - A_ROPE_DIM), F32)
    wq = jnp.concatenate([nope, rope, zpad], axis=-1).reshape(A_Q_RANK, A_PAD).astype(BF)
    rope_rot = jnp.concatenate([-rope[..., A_ROPE_DIM // 2:], rope[..., :A_ROPE_DIM // 2]], axis=-1)
    wqr = jnp.concatenate([jnp.zeros_like(nope), rope_rot, zpad], axis=-1).reshape(A_Q_RANK, A_PAD).astype(BF)
    wkv3 = w_kv_up.reshape(A_KV_RANK, A_HEADS, A_NOPE_DIM + A_V_DIM)
    k_nope, v = wkv3[..., :A_NOPE_DIM], wkv3[..., A_NOPE_DIM:]
    zk = jnp.zeros_like(k_nope)
    wk = jnp.concatenate([k_nope, zk], axis=-1).reshape(A_KV_RANK, A_PAD).astype(BF)
    v_even = jnp.concatenate([v, jnp.zeros_like(v)], axis=-1)
    v_odd = jnp.concatenate([jnp.zeros_like(v), v], axis=-1)
    odd = (np.arange(A_HEADS) % 2 == 1)[None, :, None]
    wv = jnp.where(odd, v_odd, v_even).reshape(A_KV_RANK, A_PAD).astype(BF)
    e = np.zeros((LANES, A_PAD), np.float32)
    for hd in range(A_HEADS):
        for j in range(A_ROPE_DIM):
            e[j, hd * LANES + A_NOPE_DIM + j] = 1.0
    wkv = jnp.concatenate([jnp.concatenate([wk, wv], axis=1),
                           jnp.concatenate([jnp.asarray(e, BF), jnp.zeros((LANES, A_PAD), BF)], axis=1)], axis=0)
    return dict(win=win, qg=q_norm[None, :], kvg=kv_norm[None, :], wq=wq, wqr=wqr, wkv=wkv)


def kernel(x, ev_norm, ev_w_in, ev_q_norm, ev_w_q_up, ev_kv_norm, ev_w_kv_up, ev_w_out, od_norm, od_w_qkv, od_rpb, od_w_out, ffn_norm, ffn_w_gate, ffn_w_up, ffn_w_down, final_norm):
    batch, seq, d = x.shape
    assert d == D_MODEL and seq % (DIL_QB * DIL_UNROLL) == 0 and (seq // RES_GROUPS) % DIL_QB == 0
    assert seq % (NA_QROWS * NA_UNROLL * GRID_W) == 0
    h = x.reshape(batch * seq, d)
    ta, tb = _rope_tables(seq)
    wg_all, wu_all, wd_all = ffn_w_gate.astype(BF), ffn_w_up.astype(BF), ffn_w_down.astype(BF)
    ev_wo_all, od_wo_all = ev_w_out.astype(BF), od_w_out.astype(BF)
    wqkv_all = jnp.concatenate([od_w_qkv[:, :, :C_WIDTH] * HEAD_DIM ** -0.5, od_w_qkv[:, :, C_WIDTH:]],
                               axis=2).astype(BF)
    ev_all = jax.vmap(_prep_even)(ev_w_in, ev_q_norm, ev_w_q_up, ev_kv_norm, ev_w_kv_up)
    slabs_all = jax.vmap(_na_bias_slabs)(od_rpb)
    ev_g, od_g, ffn_g = ev_norm[:, None, :], od_norm[:, None, :], ffn_norm[:, None, :]
    for layer in range(DEPTH):
        i = layer // 2
        if layer % 2 == 0:
            qa, ka, va, qb, kb, vb, qg, kg, vg = _proj_even(h, (ev_g, i), ev_all, i, ta, tb, seq)
            o_a = _mla(qa, ka, va, batch, seq)
            o_b = _dilated(qb, kb, vb, qg, kg, vg, batch, seq)
            attn, wo = [o_a, o_b], (ev_wo_all, i)
        else:
            qkv = _proj_odd(h, (od_g, i), (wqkv_all, i))
            attn = [_neighbourhood(qkv, slabs_all, i, batch, seq)]
            wo = (od_wo_all, i)
        h = _post(h, attn, wo, (ffn_g, layer), (wg_all, layer), (wu_all, layer), (wd_all, layer),
                  final_g=final_norm[None, :] if layer == DEPTH - 1 else None)
    return h.reshape(batch, seq, d)
```

```python
import functools

import numpy as np
import jax
import jax.numpy as jnp
from jax import lax
from jax.experimental import pallas as pl
from jax.experimental.pallas import tpu as pltpu

BF = jnp.bfloat16
F32 = jnp.float32

D_MODEL = 1024
DEPTH = 4
HEAD_DIM = 64
A_HEADS = 8
A_Q_RANK = 256
A_KV_RANK = 128
A_NOPE_DIM = 64
A_ROPE_DIM = 32
A_V_DIM = 64
B_HEADS = 8
B_BRANCHES = ((128, 1), (512, 4), (2048, 16))
C_HEADS = 16
GRID_W = 64
NA_ROWS = 8
NA_COLS = 16
D_FF = -(-8 * D_MODEL // (3 * 256)) * 256
ROPE_THETA = 10000.0
EPS = 1e-6
NEG_INF = -1e30
LOG2E = float(np.log2(np.e))
LN2 = float(np.log(2.0))

LANES = 128
V7X_VMEM_BYTES = 64 * 1024 * 1024
B_WIDTH = B_HEADS * HEAD_DIM
C_WIDTH = C_HEADS * HEAD_DIM
A_PAD = A_HEADS * LANES
COL_QB = A_Q_RANK + A_KV_RANK
COL_KB = COL_QB + B_WIDTH
COL_VB = COL_KB + B_WIDTH
COL_KPE = COL_VB + B_WIDTH
COL_KPE_ROT = COL_KPE + LANES
HALF_WINDOW = B_BRANCHES[0][0] // (2 * B_BRANCHES[0][1])
assert all(w // (2 * d) == HALF_WINDOW for w, d in B_BRANCHES)
NA_QROWS = 2
NA_KROWS = NA_QROWS + NA_ROWS
NA_VARIANTS = 5
NA_UNROLL = 16
RES_GROUPS = B_BRANCHES[-1][1]
DEINT = 4
assert DEINT * DEINT == RES_GROUPS
DIL_QB = 2 * HALF_WINDOW
DIL_KB = DIL_QB + 2 * HALF_WINDOW
DIL_UNROLL = 16
DIL1_MASK_BLOCKS = (0, 1, -1)
MLA_GROUP = 2
POST_SPLIT = 2
VMEM_LIMIT = V7X_VMEM_BYTES - 8 * 1024 * 1024


def _rms(x, g):
    return x * lax.rsqrt(jnp.mean(x * x, axis=-1, keepdims=True) + EPS) * g


def _dot(a, b):
    return jnp.dot(a, b, preferred_element_type=F32)


def _dot_nt(a, b):
    return lax.dot_general(a, b, (((1,), (1,)), ((), ())), preferred_element_type=F32)


def _loop(trips, body):
    if trips == 1:
        body(0, 0)
    else:
        lax.fori_loop(0, trips, body, 0)


def _clip(x, lo, hi):
    return min(max(x, lo), hi) if isinstance(x, int) else jnp.clip(x, lo, hi)


def _select(pred, a, b):
    return (a if pred else b) if isinstance(pred, bool) else jnp.where(pred, a, b)


def _aligned(x, m):
    return x if isinstance(x, int) else pl.multiple_of(x, m)


def _const_spec(shape):
    nd = len(shape)
    return pl.BlockSpec(shape, lambda *_: (0,) * nd, pipeline_mode=pl.Buffered(1))


def _weight_spec(w):
    if isinstance(w, tuple):
        stack, idx = w
        nd = stack.ndim - 1
        return pl.BlockSpec((None,) + stack.shape[1:], lambda *_: (idx,) + (0,) * nd, pipeline_mode=pl.Buffered(1))
    return _const_spec(w.shape)


def _weight_arrays(ws):
    return [w[0] if isinstance(w, tuple) else w for w in ws]


def _params(n_grid):
    return pltpu.CompilerParams(dimension_semantics=("arbitrary",) * n_grid,
                                vmem_limit_bytes=VMEM_LIMIT)


def _proj_even_kernel(h_ref, g_ref, win_ref, qg_ref, kvg_ref, wq_ref, wqr_ref, wkv_ref,
                      ta_ref, tb_ref, qa_ref, ka_ref, va_ref, qb_ref, kb_ref, vb_ref,
                      qr_ref, kr_ref, vr_ref, stage_ref, mid_ref):
    xn = _rms(h_ref[...], g_ref[...]).astype(BF)
    z = _dot(xn, win_ref[...])
    cq = _rms(z[:, 0:A_Q_RANK], qg_ref[...]).astype(BF)
    ckv = _rms(z[:, A_Q_RANK:COL_QB], kvg_ref[...]).astype(BF)
    qa = _dot(cq, wq_ref[...])
    qr = _dot(cq, wqr_ref[...])
    cq_t, sq_t = ta_ref[:, 0:LANES], ta_ref[:, LANES:2 * LANES]
    ck_t, sk_t = ta_ref[:, 2 * LANES:3 * LANES], ta_ref[:, 3 * LANES:4 * LANES]
    for hd in range(A_HEADS):
        sl = slice(hd * LANES, (hd + 1) * LANES)
        qa_ref[:, sl] = (qa[:, sl] * cq_t + qr[:, sl] * sq_t).astype(BF)
    kpe = z[:, COL_KPE:COL_KPE + LANES] * ck_t + z[:, COL_KPE_ROT:COL_KPE_ROT + LANES] * sk_t
    kv = _dot(jnp.concatenate([ckv, kpe.astype(BF)], axis=1), wkv_ref[...])
    ka_ref[...] = kv[:, 0:A_PAD].astype(BF)
    col = lax.broadcasted_iota(jnp.int32, (1, A_PAD), 1)
    own_half = ((col // HEAD_DIM) % 2) == ((col // LANES) % 2)
    va_ref[...] = (kv[:, A_PAD:] + jnp.where(own_half, 0.0, 1.0)).astype(BF)
    for p in range(B_WIDTH // LANES):
        for idx, (col, out) in enumerate(((COL_QB, qb_ref), (COL_KB, kb_ref))):
            t0 = (1 - idx) * 3 * LANES
            c_t, s_hi, s_lo = (tb_ref[:, t0 + n * LANES:t0 + (n + 1) * LANES] for n in range(3))
            xs = z[:, col + p * LANES: col + (p + 1) * LANES]
            r = (xs * c_t + pltpu.roll(xs, HEAD_DIM // 2, 1) * s_hi
                 + pltpu.roll(xs, LANES - HEAD_DIM // 2, 1) * s_lo)
            out[:, p * LANES:(p + 1) * LANES] = r.astype(BF)
            stage_ref[idx, p] = r
        stage_ref[2, p] = z[:, COL_VB + p * LANES:COL_VB + (p + 1) * LANES]
    vb_ref[...] = z[:, COL_VB:COL_KPE].astype(BF)
    tm = stage_ref.shape[2]
    for idx, out in enumerate((qr_ref, kr_ref, vr_ref)):
        for p in range(B_WIDTH // LANES):
            for b in range(DEINT):
                mid_ref[idx, p, b] = stage_ref[idx, p, pl.ds(b, tm // DEINT, stride=DEINT), :]
            for r in range(RES_GROUPS):
                out[r, :, p * LANES:(p + 1) * LANES] = (
                    mid_ref[idx, p, r % DEINT, pl.ds(r // DEINT, tm // RES_GROUPS, stride=DEINT), :].astype(BF))


def _proj_even(h, g, w, layer, ta, tb, seq, tm=512):
    T = h.shape[0]
    nseq = seq // tm
    row = lambda c: pl.BlockSpec((tm, c), lambda i: (i, 0))
    tab = lambda c: pl.BlockSpec((tm, c), lambda i: (i % nseq, 0))
    consts = [g] + [(w[name], layer) for name in ("win", "qg", "kvg", "wq", "wqr", "wkv")]
    out_cols = (A_PAD, A_PAD, A_PAD, B_WIDTH, B_WIDTH, B_WIDTH)
    grouped = pl.BlockSpec((None, RES_GROUPS, tm // RES_GROUPS, B_WIDTH), lambda i: (i // nseq, 0, i % nseq, 0))
    grouped_shape = jax.ShapeDtypeStruct((T // seq, RES_GROUPS, seq // RES_GROUPS, B_WIDTH), BF)
    return pl.pallas_call(
        _proj_even_kernel,
        grid=(T // tm,),
        in_specs=[row(D_MODEL)] + [_weight_spec(c) for c in consts] + [tab(4 * LANES), tab(6 * LANES)],
        out_specs=[row(c) for c in out_cols] + [grouped] * 3,
        out_shape=[jax.ShapeDtypeStruct((T, c), BF) for c in out_cols] + [grouped_shape] * 3,
        scratch_shapes=[pltpu.VMEM((3, B_WIDTH // LANES, tm, LANES), F32),
                        pltpu.VMEM((3, B_WIDTH // LANES, DEINT, tm // DEINT, LANES), F32)],
        compiler_params=_params(1),
        name="proj_even",
    )(h, *_weight_arrays(consts), ta, tb)


def _mla_kernel(q_ref, k_ref, v_ref, o_ref):
    low = _low_half(q_ref.shape[0])
    tile = lambda hd: slice(hd * LANES, (hd + 1) * LANES)
    for g in range(A_HEADS // MLA_GROUP):
        heads = range(g * MLA_GROUP, (g + 1) * MLA_GROUP)
        scores = [_dot_nt(q_ref[:, tile(hd)], k_ref[:, tile(hd)]) for hd in heads]
        probs = [_exp_rows(s)[0] for s in scores]
        outs = [_dot(p, v_ref[:, tile(hd)]) for p, hd in zip(probs, heads)]
        for pair in range(MLA_GROUP // 2):
            o_ref[:, tile(g * MLA_GROUP // 2 + pair)] = _pair_finish(outs[2 * pair], outs[2 * pair + 1], low).astype(BF)


def _mla(qa, ka, va, batch, seq, tq=1024):
    q3, k3, v3 = (t.reshape(batch, seq, A_PAD) for t in (qa, ka, va))
    out = pl.pallas_call(
        _mla_kernel,
        grid=(batch, seq // tq),
        in_specs=[pl.BlockSpec((None, tq, A_PAD), lambda b, i: (b, i, 0)),
                  pl.BlockSpec((None, seq, A_PAD), lambda b, i: (b, 0, 0)),
                  pl.BlockSpec((None, seq, A_PAD), lambda b, i: (b, 0, 0))],
        out_specs=pl.BlockSpec((None, tq, B_WIDTH), lambda b, i: (b, i, 0)),
        out_shape=jax.ShapeDtypeStruct((batch, seq, A_HEADS * A_V_DIM), BF),
        compiler_params=_params(2),
        name="mla_attention",
    )(q3, k3, v3)
    return out.reshape(batch * seq, A_HEADS * A_V_DIM)


def _low_half(n):
    return lax.broadcasted_iota(jnp.int32, (n, LANES), 1) < HEAD_DIM


def _pair_scores(q2, k2, bias, low_q):
    zero = jnp.zeros_like(q2)
    return [_dot_nt(jnp.where(low_q, q2, zero), k2) + bias, _dot_nt(jnp.where(low_q, zero, q2), k2) + bias]


def _pair_values(v2, low_k):
    one = jnp.ones_like(v2)
    return [jnp.where(low_k, v2, one), jnp.where(low_k, one, v2)]


def _exp_rows(s):
    m = jnp.max(s, axis=-1, keepdims=True)
    return jnp.exp2(s - m).astype(BF), m


def _pair_finish(o0, o1, low_q, m0=None, m1=None):
    den = pltpu.roll(jnp.where(low_q, o1, o0), HEAD_DIM, 1)
    out = jnp.where(low_q, o0, o1) / den
    if m0 is None:
        return out
    return out, jnp.where(low_q, m0, m1) * LN2 + jnp.log(den)


def _attend_blocks(blocks, low_q, low_k):
    scores = [_pair_scores(q2, k2, bias, low_q) for q2, k2, _, bias in blocks]
    probs = [[_exp_rows(s) for s in pair] for pair in scores]
    results = []
    for (_, _, v2, _), pair in zip(blocks, probs):
        vals = _pair_values(v2, low_k)
        o0, o1 = _dot(pair[0][0], vals[0]), _dot(pair[1][0], vals[1])
        results.append(_pair_finish(o0, o1, low_q, pair[0][1], pair[1][1]))
    return results


def _grouped_branch(qg_ref, kg_ref, vg_ref, mask_ref, mask_base, o_ref, l_ref, dil):
    n16 = qg_ref.shape[1]
    per, jq, jk, n_jb = _grouped_geometry(dil, n16)
    nq, nk = per * jq, per * jk
    low_q, low_k = _low_half(nq), _low_half(nk)
    r_unroll = max(1, DIL_UNROLL // n_jb)

    def body(i, carry):
        blocks, where = [], []
        for ru in range(r_unroll):
            rr = i * r_unroll + ru
            for jb in range(n_jb):
                q0, k0 = _grouped_window(jb, per, jq, jk, n16)
                cat = lambda ref, lo, n: jnp.concatenate(
                    [ref[rr + dil * u, lo:lo + n, :] for u in range(per)], axis=0)
                blocks.append((cat(qg_ref, q0, jq), cat(kg_ref, k0, jk), cat(vg_ref, k0, jk),
                               mask_ref[mask_base + jb, :, 0:nk]))
                where.append((rr, q0))
        for (rr, q0), (o, l) in zip(where, _attend_blocks(blocks, low_q, low_k)):
            for u in range(per):
                o_ref[rr + dil * u, q0:q0 + jq, :] = o[u * jq:(u + 1) * jq]
                l_ref[rr + dil * u, q0:q0 + jq, :] = l[u * jq:(u + 1) * jq]
        return carry

    _loop(dil // r_unroll, body)


def _grouped_geometry(dil, n16):
    per = RES_GROUPS // dil
    jq = DIL_QB // per
    jk = min(DIL_KB, n16 * per) // per
    return per, jq, jk, n16 // jq


def _grouped_window(jb, per, jq, jk, n16):
    q0 = jq * jb
    return q0, int(np.clip(q0 - HALF_WINDOW // per, 0, n16 - jk))


def _dilated_masks(seq):
    n16 = seq // RES_GROUPS
    neg = np.float32(NEG_INF)
    q = np.arange(DIL_QB)[:, None]
    k = np.arange(DIL_KB)[None, :]
    tiles = []
    n_blocks = seq // DIL_QB
    for a in DIL1_MASK_BLOCKS:
        a %= n_blocks
        qs = a * DIL_QB
        ks = int(np.clip(qs - HALF_WINDOW, 0, seq - DIL_KB))
        tiles.append(np.where(np.abs((k + ks) - (q + qs)) <= HALF_WINDOW, 0, neg))
    for _, dil in B_BRANCHES[1:]:
        per, jq, jk, n_jb = _grouped_geometry(dil, n16)
        pos = lambda i, n: per * (i % n) + i // n
        for jb in range(n_jb):
            q0, k0 = _grouped_window(jb, per, jq, jk, n16)
            rel = (pos(k, jk) + per * k0) - (pos(q, jq) + per * q0)
            tile = np.where(np.abs(rel) <= HALF_WINDOW, 0, neg)
            tile[:, per * jk:] = neg
            tiles.append(tile)
    return jnp.asarray(np.stack(tiles).astype(np.float32))


def _dilated_kernel(qn_ref, kn_ref, vn_ref, qg_ref, kg_ref, vg_ref, mask_ref, out_ref,
                    o1_ref, l1_ref, o2_ref, l2_ref, o3_ref, l3_ref, mix_ref, om_ref, lm_ref, mm_ref):
    seq = qn_ref.shape[0]
    n16 = seq // RES_GROUPS
    n_blocks = seq // DIL_QB
    low_q, low_k = _low_half(DIL_QB), _low_half(DIL_KB)

    def body(i, carry):
        blocks, starts = [], []
        for u in range(DIL_UNROLL):
            a = i * DIL_UNROLL + u
            qs = _aligned(a * DIL_QB, DIL_QB)
            ks = _aligned(_clip(a * DIL_QB - HALF_WINDOW, 0, seq - DIL_KB), HALF_WINDOW)
            variant = _select(a == 0, 0, _select(a == n_blocks - 1, 2, 1))
            blocks.append((qn_ref[pl.ds(qs, DIL_QB), :], kn_ref[pl.ds(ks, DIL_KB), :],
                           vn_ref[pl.ds(ks, DIL_KB), :], mask_ref[variant]))
            starts.append(qs)
        for qs, (o, l) in zip(starts, _attend_blocks(blocks, low_q, low_k)):
            o1_ref[pl.ds(qs, DIL_QB), :] = o
            l1_ref[pl.ds(qs, DIL_QB), :] = l
        return carry

    _loop(n_blocks // DIL_UNROLL, body)
    base4 = len(DIL1_MASK_BLOCKS)
    base16 = base4 + _grouped_geometry(B_BRANCHES[1][1], n16)[3]
    _grouped_branch(qg_ref, kg_ref, vg_ref, mask_ref, base4, o2_ref, l2_ref, B_BRANCHES[1][1])
    _grouped_branch(qg_ref, kg_ref, vg_ref, mask_ref, base16, o3_ref, l3_ref, B_BRANCHES[2][1])
    for b in range(DEINT):
        quarter = pl.ds(b, seq // DEINT, stride=DEINT)
        om_ref[b] = o1_ref[quarter, :]
        lm_ref[b] = l1_ref[quarter, :]
    for r in range(RES_GROUPS):
        rows = pl.ds(r // DEINT, n16, stride=DEINT)
        la, lb, lc = lm_ref[r % DEINT, rows, :], l2_ref[r], l3_ref[r]
        m = jnp.maximum(jnp.maximum(la, lb), lc)
        ea, eb, ec = jnp.exp(la - m), jnp.exp(lb - m), jnp.exp(lc - m)
        mm_ref[r % DEINT, rows, :] = (ea * om_ref[r % DEINT, rows, :] + eb * o2_ref[r] + ec * o3_ref[r]) / (ea + eb + ec)
    for b in range(DEINT):
        mix_ref[pl.ds(b, seq // DEINT, stride=DEINT), :] = mm_ref[b]
    out_ref[...] = mix_ref[...].astype(BF)


def _dilated(qn, kn, vn, qg, kg, vg, batch, seq):
    n16 = seq // RES_GROUPS
    nat = pl.BlockSpec((None, seq, LANES), lambda b, hp: (b, 0, hp))
    grp = pl.BlockSpec((None, RES_GROUPS, n16, LANES), lambda b, hp: (b, 0, 0, hp))
    nat3 = lambda t: t.reshape(batch, seq, B_WIDTH)
    masks = _dilated_masks(seq)
    out = pl.pallas_call(
        _dilated_kernel,
        grid=(batch, B_WIDTH // LANES),
        in_specs=[nat, nat, nat, grp, grp, grp, _const_spec(masks.shape)],
        out_specs=nat,
        out_shape=jax.ShapeDtypeStruct((batch, seq, B_WIDTH), BF),
        scratch_shapes=[pltpu.VMEM((seq, LANES), F32), pltpu.VMEM((seq, LANES), F32)]
        + [pltpu.VMEM((RES_GROUPS, n16, LANES), F32)] * 4 + [pltpu.VMEM((seq, LANES), F32)]
        + [pltpu.VMEM((DEINT, seq // DEINT, LANES), F32)] * 3,
        compiler_params=_params(2),
        name="dilated_attention",
    )(nat3(qn), nat3(kn), nat3(vn), qg, kg, vg, masks)
    return out.reshape(batch * seq, B_WIDTH)


def _odd_projection(h, g_ref, w_ref, o_ref, rows=slice(None)):
    xn = _rms(h, g_ref[...]).astype(BF)
    z = _dot(xn, w_ref[...])
    o_ref[rows, 0:C_WIDTH] = (z[:, 0:C_WIDTH] * LOG2E).astype(BF)
    o_ref[rows, C_WIDTH:2 * C_WIDTH] = z[:, C_WIDTH:2 * C_WIDTH].astype(BF)
    vals = _pair_values(z[:, 2 * C_WIDTH:], (lax.broadcasted_iota(jnp.int32, (1, C_WIDTH), 1) % LANES) < HEAD_DIM)
    o_ref[rows, 2 * C_WIDTH:3 * C_WIDTH] = vals[0].astype(BF)
    o_ref[rows, 3 * C_WIDTH:] = vals[1].astype(BF)


def _proj_odd_kernel(h_ref, g_ref, w_ref, o_ref):
    _odd_projection(h_ref[...], g_ref, w_ref, o_ref)


def _proj_odd(h, g, w, tm=512):
    T = h.shape[0]
    n = w[0].shape[-1] + C_WIDTH
    return pl.pallas_call(
        _proj_odd_kernel,
        grid=(T // tm,),
        in_specs=[pl.BlockSpec((tm, D_MODEL), lambda i: (i, 0)), _weight_spec(g), _weight_spec(w)],
        out_specs=pl.BlockSpec((tm, n), lambda i: (i, 0)),
        out_shape=jax.ShapeDtypeStruct((T, n), BF),
        compiler_params=_params(1),
        name="proj_odd",
    )(h, *_weight_arrays([g, w]))


def _na_kernel(q_ref, k_ref, v0_ref, v1_ref, slab_ref, o_ref, b_ref, *, seq):
    rows = seq // GRID_W
    nq = NA_QROWS * GRID_W
    nk = NA_KROWS * GRID_W
    low = _low_half(nq)

    @pl.when(pl.program_id(1) == 0)
    def _():
        low_c = _low_half(GRID_W)
        index = _na_slab_index(seq)
        for v, qr, kp in np.ndindex(NA_VARIANTS, NA_QROWS, NA_KROWS // 2):
            i0, i1 = int(index[v, qr, 2 * kp]), int(index[v, qr, 2 * kp + 1])
            for j in range(2):
                b_ref[j, v, qr * GRID_W:(qr + 1) * GRID_W, kp * LANES:(kp + 1) * LANES] = jnp.where(
                    low_c, slab_ref[j, i0], slab_ref[j, i1])

    n_blocks = rows // NA_QROWS

    def body(i, carry):
        scores, where = [], []
        for u in range(NA_UNROLL):
            a = i * NA_UNROLL + u
            qs = _aligned(a * nq, nq)
            ks = _aligned(_clip(NA_QROWS * a - NA_ROWS // 2, 0, rows - NA_KROWS) * GRID_W, GRID_W)
            var = _select(a < 2, a, _select(a >= n_blocks - 2, a - (n_blocks - NA_VARIANTS), 2))
            q2 = q_ref[pl.ds(qs, nq), :]
            k2 = k_ref[pl.ds(ks, nk), :]
            zero = jnp.zeros_like(q2)
            scores.append([_dot_nt(jnp.where(low, q2, zero), k2) + b_ref[0, var],
                           _dot_nt(jnp.where(low, zero, q2), k2) + b_ref[1, var]])
            where.append((qs, ks))
        probs = [[_exp_rows(s)[0] for s in pair] for pair in scores]
        for (qs, ks), pair in zip(where, probs):
            o0 = _dot(pair[0], v0_ref[pl.ds(ks, nk), :])
            o1 = _dot(pair[1], v1_ref[pl.ds(ks, nk), :])
            o_ref[pl.ds(qs, nq), :] = _pair_finish(o0, o1, low).astype(BF)
        return carry

    _loop(n_blocks // NA_UNROLL, body)


def _na_slab_index(seq):
    rows = seq // GRID_W
    n_blocks = rows // NA_QROWS
    n_off = 2 * NA_ROWS - 1
    idx = np.full((NA_VARIANTS, NA_QROWS, NA_KROWS), n_off, np.int32)
    for v, a in enumerate((0, 1, 2, n_blocks - 2, n_blocks - 1)):
        ws = int(np.clip(NA_QROWS * a - NA_ROWS // 2, 0, rows - NA_KROWS))
        for qr in range(NA_QROWS):
            r = NA_QROWS * a + qr
            rs = int(np.clip(r - NA_ROWS // 2, 0, rows - NA_ROWS))
            for kr in range(NA_KROWS):
                krow = ws + kr
                if rs <= krow < rs + NA_ROWS:
                    idx[v, qr, kr] = krow - r + NA_ROWS - 1
    return idx


def _na_bias_slabs(rpb):
    col = np.arange(GRID_W)
    cs = np.clip(col - NA_COLS // 2, 0, GRID_W - NA_COLS)
    col_valid = (col[None, :] >= cs[:, None]) & (col[None, :] < cs[:, None] + NA_COLS)
    col_off = np.clip(col[None, :] - col[:, None] + NA_COLS - 1, 0, 2 * NA_COLS - 2)
    onehot = (col_off[..., None] == np.arange(2 * NA_COLS - 1)).astype(np.float32)
    toeplitz = jnp.einsum("hrd,ckd->hrck", rpb, onehot, precision=lax.Precision.HIGHEST)
    t2 = jnp.where(col_valid[None, None], toeplitz * LOG2E, NEG_INF)
    t2 = jnp.concatenate([t2, jnp.full_like(t2[:, :1], NEG_INF)], axis=1)
    return jnp.concatenate([t2, t2], axis=-1)


def _neighbourhood(qkv, slabs, layer, batch, seq):
    qkv3 = qkv.reshape(batch, seq, 4 * C_WIDTH)
    n_hp = C_WIDTH // LANES
    blk = lambda off: pl.BlockSpec((None, seq, LANES), lambda hp, b: (b, 0, off + hp))
    out = pl.pallas_call(
        functools.partial(_na_kernel, seq=seq),
        grid=(n_hp, batch),
        in_specs=[blk(0), blk(n_hp), blk(2 * n_hp), blk(3 * n_hp),
                  pl.BlockSpec((None, 2) + slabs.shape[2:], lambda hp, b: (layer, hp, 0, 0, 0))],
        out_specs=pl.BlockSpec((None, seq, LANES), lambda hp, b: (b, 0, hp)),
        out_shape=jax.ShapeDtypeStruct((batch, seq, C_WIDTH), BF),
        scratch_shapes=[pltpu.VMEM((2, NA_VARIANTS, NA_QROWS * GRID_W, NA_KROWS * GRID_W), F32)],
        compiler_params=_params(2),
        name="neighbourhood_attention",
    )(qkv3, qkv3, qkv3, qkv3, slabs)
    return out.reshape(batch * seq, C_WIDTH)


def _post_kernel(*refs, n_attn, final, fuse_next):
    h_ref = refs[0]
    attn = refs[1:1 + n_attn]
    wo_ref, g_ref, wg_ref, wu_ref, wd_ref = refs[1 + n_attn:6 + n_attn]
    fg_ref = refs[6 + n_attn] if final else None
    if fuse_next:
        ng_ref, nw_ref = refs[6 + n_attn:8 + n_attn]
        out_ref, qkv_ref = refs[-2:]
    else:
        out_ref = refs[-1]
    tm = h_ref.shape[0]
    parts = [slice(i * tm // POST_SPLIT, (i + 1) * tm // POST_SPLIT) for i in range(POST_SPLIT)]
    h1s = []
    for rows in parts:
        h1 = h_ref[rows, :]
        off = 0
        for a in attn:
            width = a.shape[1]
            h1 = h1 + _dot(a[rows, :], wo_ref[off:off + width, :])
            off += width
        h1s.append(h1)
    xns = [_rms(h1, g_ref[...]).astype(BF) for h1 in h1s]
    gates = [_dot(xn, wg_ref[...]) for xn in xns]
    ups = [_dot(xn, wu_ref[...]) for xn in xns]
    acts = [(gate * (1.0 / (1.0 + jnp.exp(-gate))) * up).astype(BF) for gate, up in zip(gates, ups)]
    for rows, h1, act in zip(parts, h1s, acts):
        y = h1 + _dot(act, wd_ref[...])
        if final:
            y = _rms(y, fg_ref[...])
        out_ref[rows, :] = y
        if fuse_next:
            _odd_projection(y, ng_ref, nw_ref, qkv_ref, rows)


def _post(h, attn, wo, g, wg, wu, wd, final_g=None, next_proj=None, tm=512):
    T = h.shape[0]
    final = final_g is not None
    fuse_next = next_proj is not None
    row = lambda c: pl.BlockSpec((tm, c), lambda i: (i, 0))
    consts = [wo, g, wg, wu, wd] + ([final_g] if final else []) + (list(next_proj) if fuse_next else [])
    out_specs, out_shape = [row(D_MODEL)], [jax.ShapeDtypeStruct((T, D_MODEL), F32)]
    if fuse_next:
        out_specs.append(row(4 * C_WIDTH))
        out_shape.append(jax.ShapeDtypeStruct((T, 4 * C_WIDTH), BF))
    outs = pl.pallas_call(
        functools.partial(_post_kernel, n_attn=len(attn), final=final, fuse_next=fuse_next),
        grid=(T // tm,),
        in_specs=[row(D_MODEL)] + [row(a.shape[1]) for a in attn] + [_weight_spec(c) for c in consts],
        out_specs=out_specs,
        out_shape=out_shape,
        compiler_params=_params(1),
        name="outproj_ffn",
    )(h, *attn, *_weight_arrays(consts))
    return outs if fuse_next else outs[0]


def _rope_tables(seq):
    pos = jnp.arange(seq, dtype=F32)

    def cs(dim):
        inv = ROPE_THETA ** (-jnp.arange(0, dim, 2, dtype=F32) / dim)
        ang = pos[:, None] * inv[None, :]
        return jnp.cos(ang), jnp.sin(ang)

    cos_a, sin_a = cs(A_ROPE_DIM)
    cos2, sin2 = jnp.tile(cos_a, (1, 2)), jnp.tile(sin_a, (1, 2))
    zeros = lambda n: jnp.zeros((seq, n), F32)
    scale = (A_NOPE_DIM + A_ROPE_DIM) ** -0.5 * LOG2E
    pad = LANES - A_NOPE_DIM - A_ROPE_DIM
    cq = jnp.concatenate([jnp.full((seq, A_NOPE_DIM), scale, F32), cos2 * scale, zeros(pad)], axis=1)
    sq = jnp.concatenate([zeros(A_NOPE_DIM), sin2 * scale, zeros(pad)], axis=1)
    ck = jnp.concatenate([cos2, zeros(LANES - A_ROPE_DIM)], axis=1)
    sk = jnp.concatenate([sin2, zeros(LANES - A_ROPE_DIM)], axis=1)
    ta = jnp.concatenate([cq, sq, ck, sk], axis=1)
    cos_b, sin_b = cs(HEAD_DIM)
    zb = jnp.zeros_like(sin_b)
    c = jnp.tile(cos_b, (1, 4))
    s_hi = jnp.tile(jnp.concatenate([zb, sin_b], axis=1), (1, 2))
    s_lo = jnp.tile(jnp.concatenate([-sin_b, zb], axis=1), (1, 2))
    tb = jnp.concatenate([c, s_hi, s_lo], axis=1)
    tb = jnp.concatenate([tb, tb * LOG2E], axis=1)
    return ta, tb


def _rot_half_cols(w):
    half = w.shape[1] // 2
    return jnp.concatenate([-w[:, half:], w[:, :half]], axis=1)


def _prep_even(w_in, q_norm, w_q_up, kv_norm, w_kv_up):
    d = w_in.shape[0]
    c_q, c_kv, k_pe, qkv_b = jnp.split(w_in, [A_Q_RANK, COL_QB, COL_QB + A_ROPE_DIM], axis=1)
    q_b, k_b, v_b = jnp.split(qkv_b, 3, axis=1)
    padc = jnp.zeros((d, LANES - A_ROPE_DIM), F32)
    win = jnp.concatenate([c_q, c_kv, q_b * HEAD_DIM ** -0.5, k_b, v_b,
                           k_pe, padc, _rot_half_cols(k_pe), padc], axis=1).astype(BF)
    wq3 = w_q_up.reshape(A_Q_RANK, A_HEADS, A_NOPE_DIM + A_ROPE_DIM)
    nope, rope = wq3[..., :A_NOPE_DIM], wq3[..., A_NOPE_DIM:]
    zpad = jnp.zeros((A_Q_RANK, A_HEADS, LANES - A_NOPE_DIM - A_ROPE_DIM), F32)
    wq = jnp.concatenate([nope, rope, zpad], axis=-1).reshape(A_Q_RANK, A_PAD).astype(BF)
    rope_rot = jnp.concatenate([-rope[..., A_ROPE_DIM // 2:], rope[..., :A_ROPE_DIM // 2]], axis=-1)
    wqr = jnp.concatenate([jnp.zeros_like(nope), rope_rot, zpad], axis=-1).reshape(A_Q_RANK, A_PAD).astype(BF)
    wkv3 = w_kv_up.reshape(A_KV_RANK, A_HEADS, A_NOPE_DIM + A_V_DIM)
    k_nope, v = wkv3[..., :A_NOPE_DIM], wkv3[..., A_NOPE_DIM:]
    zk = jnp.zeros_like(k_nope)
    wk = jnp.concatenate([k_nope, zk], axis=-1).reshape(A_KV_RANK, A_PAD).astype(BF)
    v_even = jnp.concatenate([v, jnp.zeros_like(v)], axis=-1)
    v_odd = jnp.concatenate([jnp.zeros_like(v), v], axis=-1)
    odd = (np.arange(A_HEADS) % 2 == 1)[None, :, None]
    wv = jnp.where(odd, v_odd, v_even).reshape(A_KV_RANK, A_PAD).astype(BF)
    e = np.zeros((LANES, A_PAD), np.float32)
    for hd in range(A_HEADS):
        for j in range(A_ROPE_DIM):
            e[j, hd * LANES + A_NOPE_DIM + j] = 1.0
    wkv = jnp.concatenate([jnp.concatenate([wk, wv], axis=1),
                           jnp.concatenate([jnp.asarray(e, BF), jnp.zeros((LANES, A_PAD), BF)], axis=1)], axis=0)
    return dict(win=win, qg=q_norm[None, :], kvg=kv_norm[None, :], wq=wq, wqr=wqr, wkv=wkv)


def kernel(x, ev_norm, ev_w_in, ev_q_norm, ev_w_q_up, ev_kv_norm, ev_w_kv_up, ev_w_out, od_norm, od_w_qkv, od_rpb, od_w_out, ffn_norm, ffn_w_gate, ffn_w_up, ffn_w_down, final_norm):
    batch, seq, d = x.shape
    assert d == D_MODEL and seq % (DIL_QB * DIL_UNROLL) == 0 and (seq // RES_GROUPS) % DIL_QB == 0
    assert seq % (NA_QROWS * NA_UNROLL * GRID_W) == 0
    h = x.reshape(batch * seq, d)
    ta, tb = _rope_tables(seq)
    wg_all, wu_all, wd_all = ffn_w_gate.astype(BF), ffn_w_up.astype(BF), ffn_w_down.astype(BF)
    ev_wo_all, od_wo_all = ev_w_out.astype(BF), od_w_out.astype(BF)
    wqkv_all = jnp.concatenate([od_w_qkv[:, :, :C_WIDTH] * HEAD_DIM ** -0.5, od_w_qkv[:, :, C_WIDTH:]],
                               axis=2).astype(BF)
    ev_all = jax.vmap(_prep_even)(ev_w_in, ev_q_norm, ev_w_q_up, ev_kv_norm, ev_w_kv_up)
    slabs_all = jax.vmap(_na_bias_slabs)(od_rpb)
    ev_g, od_g, ffn_g = ev_norm[:, None, :], od_norm[:, None, :], ffn_norm[:, None, :]
    qkv = None
    for layer in range(DEPTH):
        i = layer // 2
        if layer % 2 == 0:
            qa, ka, va, qb, kb, vb, qg, kg, vg = _proj_even(h, (ev_g, i), ev_all, i, ta, tb, seq)
            o_a = _mla(qa, ka, va, batch, seq)
            o_b = _dilated(qb, kb, vb, qg, kg, vg, batch, seq)
            attn, wo = [o_a, o_b], (ev_wo_all, i)
        else:
            if qkv is None:
                qkv = _proj_odd(h, (od_g, i), (wqkv_all, i))
            attn = [_neighbourhood(qkv, slabs_all, i, batch, seq)]
            wo = (od_wo_all, i)
        fuse = layer % 2 == 0 and layer + 1 < DEPTH
        out = _post(h, attn, wo, (ffn_g, layer), (wg_all, layer), (wu_all, layer), (wd_all, layer),
                    final_g=final_norm[None, :] if layer == DEPTH - 1 else None,
                    next_proj=((od_g, i), (wqkv_all, i)) if fuse else None)
        h, qkv = out if fuse else (out, None)
    return h.reshape(batch, seq, d)
```
